```python
import math
import jax, jax.numpy as jnp
from jax import lax
import numpy as np

D_MODEL = 2048
BATCH = 2
SEQ = 8192
DEPTH = 2
DEC_BATCH = 8
DEC_SEQ = 4096
PAST_LEN = 128

N_META = 16
MIX_WIDTH = D_MODEL
F_GROUPS = 4
F_GROUP_DIM = 128
F_WIDTH = F_GROUPS * F_GROUP_DIM
MLA_HEADS = 8
Q_LORA = 768
KV_LORA = 512
QK_NOPE = 128
QK_ROPE = 64
V_HEAD = 128
MLA_WIDTH = MLA_HEADS * V_HEAD
ROPE_THETA = 10000.0
DIFF_HEADS = 4
DIFF_QK = 64
DIFF_V = 2 * DIFF_QK
DIFF_WIDTH = DIFF_HEADS * DIFF_V
REL_BUCKETS = 32
REL_MAX_DIST = 128
Q_BLOCK = 128
NORM_EPS = 1e-6

IN_SIZES = [F_WIDTH, Q_LORA, KV_LORA, QK_ROPE,
            DIFF_HEADS * 2 * DIFF_QK, DIFF_HEADS * 2 * DIFF_QK, DIFF_WIDTH, MIX_WIDTH]
IN_WIDTH = sum(IN_SIZES)
IN_SPLITS = [int(v) for v in np.cumsum(IN_SIZES[:-1])]

kernel_name = 'hybrid_fnet_mla_diffattn_encoder'


def rmsnorm(x, g, eps=NORM_EPS):
    xf = x.astype(jnp.float32)
    y = xf * lax.rsqrt(jnp.mean(xf * xf, axis=-1, keepdims=True) + eps)
    return (y * g.astype(jnp.float32)).astype(x.dtype)


def rope_tables(seq_len):
    pos = jnp.arange(seq_len, dtype=jnp.float32)
    inv_freq = ROPE_THETA ** (-jnp.arange(0, QK_ROPE, 2, dtype=jnp.float32) / QK_ROPE)
    ang = pos[:, None] * inv_freq[None, :]
    return jnp.cos(ang), jnp.sin(ang)


def apply_rope(x, cos, sin):
    xf = x.astype(jnp.float32)
    x1, x2 = xf[..., :QK_ROPE // 2], xf[..., QK_ROPE // 2:]
    return jnp.concatenate([x1 * cos - x2 * sin, x2 * cos + x1 * sin], axis=-1).astype(x.dtype)


def t5_bucket(rel):
    nb = REL_BUCKETS // 2
    max_exact = nb // 2
    ret = (rel > 0).astype(jnp.int32) * nb
    n = jnp.abs(rel)
    nf = jnp.maximum(n, 1).astype(jnp.float32)
    large = max_exact + (jnp.log(nf / max_exact) / math.log(REL_MAX_DIST / max_exact)
                         * (nb - max_exact)).astype(jnp.int32)
    large = jnp.minimum(large, nb - 1)
    return ret + jnp.where(n < max_exact, n, large)


def sweep_queries(attend, qs, seq_len):
    pos = jnp.arange(seq_len, dtype=jnp.int32)
    head = attend(tuple(q[:, :N_META] for q in qs), pos[:N_META])
    n_blk = (seq_len - N_META) // Q_BLOCK

    def to_blocks(q):
        r = q[:, N_META:]
        r = r.reshape((r.shape[0], n_blk, Q_BLOCK) + r.shape[2:])
        return jnp.moveaxis(r, 1, 0)

    blks = tuple(to_blocks(q) for q in qs)
    pos_b = pos[N_META:].reshape(n_blk, Q_BLOCK)
    out = lax.map(lambda a: attend(a[0], a[1]), (blks, pos_b))
    out = jnp.moveaxis(out, 0, 1)
    out = out.reshape((out.shape[0], n_blk * Q_BLOCK) + out.shape[3:])
    return jnp.concatenate([head, out], axis=1)


def mixer_layer(x, l, rel_bias, norm_w, w_in, w_fmix, q_norm, w_uq, kv_norm, w_ukv,
                lam_q1, lam_k1, lam_q2, lam_k2, diff_norm, w_o):
    B, S, _ = x.shape
    dt = x.dtype
    h = rmsnorm(x, norm_w[l])
    proj = jnp.einsum('bsd,de->bse', h, w_in[l])
    u_f, c_q, c_kv, k_r, q_d, k_d, v_d, gate = jnp.split(proj, IN_SPLITS, axis=-1)

    uf = u_f.reshape(B, S, F_GROUPS, F_GROUP_DIM).astype(jnp.float32)
    f = jnp.fft.fft2(uf, axes=(1, 3), norm='ortho').real.astype(dt)
    y_f = jnp.einsum('bsgc,gcd->bsgd', f, w_fmix[l]).reshape(B, S, F_WIDTH)

    cos, sin = rope_tables(S)
    cq = rmsnorm(c_q, q_norm[l])
    q = jnp.einsum('bsr,re->bse', cq, w_uq[l]).reshape(B, S, MLA_HEADS, QK_NOPE + QK_ROPE)
    q_nope = q[..., :QK_NOPE]
    q_rope = apply_rope(q[..., QK_NOPE:], cos[:, None, :], sin[:, None, :])
    ckv = rmsnorm(c_kv, kv_norm[l])
    kv = jnp.einsum('bsr,re->bse', ckv, w_ukv[l]).reshape(B, S, MLA_HEADS, QK_NOPE + V_HEAD)
    k_nope = kv[..., :QK_NOPE]
    v_mla = kv[..., QK_NOPE:]
    k_rope = apply_rope(k_r, cos, sin)
    mla_scale = 1.0 / math.sqrt(QK_NOPE + QK_ROPE)

    def attend_mla(qb, pb):
        qn, qr = qb
        s = (jnp.einsum('bthd,bshd->bhts', qn, k_nope)
             + jnp.einsum('bthd,bsd->bhts', qr, k_rope)).astype(jnp.float32) * mla_scale
        p = jax.nn.softmax(s, axis=-1)
        return jnp.einsum('bhts,bshd->bthd', p.astype(dt), v_mla)

    y_mla = sweep_queries(attend_mla, (q_nope, q_rope), S).reshape(B, S, MLA_WIDTH)

    qd = q_d.reshape(B, S, DIFF_HEADS, 2, DIFF_QK)
    kd = k_d.reshape(B, S, DIFF_HEADS, 2, DIFF_QK)
    q1, q2 = qd[..., 0, :], qd[..., 1, :]
    k1, k2 = kd[..., 0, :], kd[..., 1, :]
    vd = v_d.reshape(B, S, DIFF_HEADS, DIFF_V)
    lam_init = 0.8 - 0.6 * math.exp(-0.3 * l)
    lam = (jnp.exp(jnp.sum(lam_q1[l].astype(jnp.float32) * lam_k1[l].astype(jnp.float32)))
           - jnp.exp(jnp.sum(lam_q2[l].astype(jnp.float32) * lam_k2[l].astype(jnp.float32)))
           + lam_init)
    diff_scale = 1.0 / math.sqrt(DIFF_QK)
    kpos = jnp.arange(S, dtype=jnp.int32)
    table = rel_bias.astype(jnp.float32)

    def attend_diff(qb, pb):
        qa, qb2 = qb
        bucket = t5_bucket(kpos[None, :] - pb[:, None])
        bias = jnp.transpose(table[bucket], (2, 0, 1))[None]
        s1 = jnp.einsum('bthd,bshd->bhts', qa, k1).astype(jnp.float32) * diff_scale + bias
        s2 = jnp.einsum('bthd,bshd->bhts', qb2, k2).astype(jnp.float32) * diff_scale + bias
        a = jax.nn.softmax(s1, axis=-1) - lam * jax.nn.softmax(s2, axis=-1)
        return jnp.einsum('bhts,bshd->bthd', a.astype(dt), vd)

    o_d = sweep_queries(attend_diff, (q1, q2), S)
    o_d = rmsnorm(o_d, diff_norm[l], eps=1e-5) * jnp.asarray(1.0 - lam_init, dt)
    y_d = o_d.reshape(B, S, DIFF_WIDTH)

    y = jnp.concatenate([y_f, y_mla, y_d], axis=-1) * jax.nn.silu(gate)
    return x + jnp.einsum('bse,ed->bsd', y, w_o[l])


def encode(x, meta_tokens, rel_bias, final_norm, norm_w, w_in, w_fmix, q_norm, w_uq,
           kv_norm, w_ukv, lam_q1, lam_k1, lam_q2, lam_k2, diff_norm, w_o):
    B = x.shape[0]
    meta = jnp.broadcast_to(meta_tokens.astype(x.dtype)[None], (B, N_META, D_MODEL))
    h = jnp.concatenate([meta, x], axis=1)
    for l in range(DEPTH):
        h = mixer_layer(h, l, rel_bias, norm_w, w_in, w_fmix, q_norm, w_uq, kv_norm, w_ukv,
                        lam_q1, lam_k1, lam_q2, lam_k2, diff_norm, w_o)
    h = rmsnorm(h, final_norm)
    return h[:, N_META:]


def setup_inputs(seed: int = 0) -> dict:
    key = jax.random.key(seed)
    ks = jax.random.split(key, 20)
    nrm = jax.random.normal
    f32 = jnp.float32
    return {
        'x_prompt': nrm(ks[0], (BATCH, SEQ, D_MODEL), f32),
        'x_sample': nrm(ks[1], (DEC_BATCH, DEC_SEQ, D_MODEL), f32),
        'meta_tokens': nrm(ks[2], (N_META, D_MODEL), f32),
        'rel_bias': 0.1 * nrm(ks[3], (REL_BUCKETS, DIFF_HEADS), f32),
        'final_norm': 1.0 + 0.01 * nrm(ks[4], (D_MODEL,), f32),
        'norm_w': 1.0 + 0.01 * nrm(ks[5], (DEPTH, D_MODEL), f32),
        'w_in': nrm(ks[6], (DEPTH, D_MODEL, IN_WIDTH), f32) * D_MODEL ** -0.5,
        'w_fmix': nrm(ks[7], (DEPTH, F_GROUPS, F_GROUP_DIM, F_GROUP_DIM), f32) * F_GROUP_DIM ** -0.5,
        'q_norm': 1.0 + 0.01 * nrm(ks[8], (DEPTH, Q_LORA), f32),
        'w_uq': nrm(ks[9], (DEPTH, Q_LORA, MLA_HEADS * (QK_NOPE + QK_ROPE)), f32) * Q_LORA ** -0.5,
        'kv_norm': 1.0 + 0.01 * nrm(ks[10], (DEPTH, KV_LORA), f32),
        'w_ukv': nrm(ks[11], (DEPTH, KV_LORA, MLA_HEADS * (QK_NOPE + V_HEAD)), f32) * KV_LORA ** -0.5,
        'lam_q1': 0.1 * nrm(ks[12], (DEPTH, DIFF_QK), f32),
        'lam_k1': 0.1 * nrm(ks[13], (DEPTH, DIFF_QK), f32),
        'lam_q2': 0.1 * nrm(ks[14], (DEPTH, DIFF_QK), f32),
        'lam_k2': 0.1 * nrm(ks[15], (DEPTH, DIFF_QK), f32),
        'diff_norm': 1.0 + 0.01 * nrm(ks[16], (DEPTH, DIFF_V), f32),
        'w_o': nrm(ks[17], (DEPTH, MIX_WIDTH, D_MODEL), f32) * MIX_WIDTH ** -0.5,
    }


def reference(x_prompt, x_sample, meta_tokens, rel_bias, final_norm, norm_w, w_in, w_fmix,
              q_norm, w_uq, kv_norm, w_ukv, lam_q1, lam_k1, lam_q2, lam_k2, diff_norm, w_o):
    y_prompt = encode(x_prompt, meta_tokens, rel_bias, final_norm, norm_w, w_in, w_fmix,
                      q_norm, w_uq, kv_norm, w_ukv, lam_q1, lam_k1, lam_q2, lam_k2, diff_norm, w_o)
    y_sample = encode(x_sample, meta_tokens, rel_bias, final_norm, norm_w, w_in, w_fmix,
                      q_norm, w_uq, kv_norm, w_ukv, lam_q1, lam_k1, lam_q2, lam_k2, diff_norm, w_o)
    return (y_prompt, y_sample)
```

```python
import functools
import math

import jax
import jax.numpy as jnp
import numpy as np
from jax import lax
from jax.experimental import pallas as pl
from jax.experimental.pallas import tpu as pltpu

D_MODEL = 2048
DEPTH = 2
N_META = 16
F_GROUPS = 4
F_GROUP_DIM = 128
F_WIDTH = F_GROUPS * F_GROUP_DIM
MLA_HEADS = 8
Q_LORA = 768
KV_LORA = 512
QK_NOPE = 128
QK_ROPE = 64
V_HEAD = 128
MLA_WIDTH = MLA_HEADS * V_HEAD
ROPE_THETA = 10000.0
DIFF_HEADS = 4
DIFF_QK = 64
DIFF_V = 2 * DIFF_QK
DIFF_WIDTH = DIFF_HEADS * DIFF_V
REL_BUCKETS = 32
REL_MAX_DIST = 128
NORM_EPS = 1e-6
DIFF_NORM_EPS = 1e-5
GATE_WIDTH = D_MODEL

LOG2E = math.log2(math.e)
MLA_QSCALE = LOG2E / math.sqrt(QK_NOPE + QK_ROPE)
DIFF_QSCALE = LOG2E / math.sqrt(DIFF_QK)
MASK_VALUE = -1e30

LANES = 128
VMEM_BYTES_V7X = 64 * 1024 * 1024

C_UF = 0
C_CQ = C_UF + F_WIDTH
C_CKV = C_CQ + Q_LORA
C_KR = C_CKV + KV_LORA
C_QD = C_KR + 2 * QK_ROPE
C_KD = C_QD + DIFF_WIDTH
C_VD = C_KD + DIFF_WIDTH
C_GATE = C_VD + DIFF_WIDTH
IN_COLS = C_GATE + GATE_WIDTH

SEQ_TILE = 256
ATT_TILE = 256
DFT_MAX_CHUNK = 2304


def _vmem_limit(nbytes):
    return int(min(nbytes, VMEM_BYTES_V7X - (8 << 20)))


def _rms(x, g, eps):
    ms = jnp.mean(x * x, axis=-1, keepdims=True)
    return x * lax.rsqrt(ms + eps) * g


def _rope_fold(v, t1):
    w = v * t1
    return w + pltpu.roll(w, QK_ROPE, 1)


def _fold_fmix_kernel(cs_ref, w_ref, o_ref):
    w = w_ref[0, 0]
    o_ref[0, 0, :, :F_GROUP_DIM] = jnp.dot(
        cs_ref[0], w, preferred_element_type=jnp.float32,
        precision=lax.Precision.HIGHEST)
    o_ref[0, 0, :, F_GROUP_DIM:] = jnp.dot(
        cs_ref[1], w, preferred_element_type=jnp.float32,
        precision=lax.Precision.HIGHEST)


def _fold_fmix(chan_cs, w_fmix):
    c = F_GROUP_DIM
    return pl.pallas_call(
        _fold_fmix_kernel,
        out_shape=jax.ShapeDtypeStruct((DEPTH, F_GROUPS, c, 2 * c), jnp.float32),
        grid=(DEPTH, F_GROUPS),
        in_specs=[pl.BlockSpec((2, c, c), lambda l, g: (0, 0, 0)),
                  pl.BlockSpec((1, 1, c, c), lambda l, g: (l, g, 0, 0))],
        out_specs=pl.BlockSpec((1, 1, c, 2 * c), lambda l, g: (l, g, 0, 0)),
        name="fold_fmix",
    )(chan_cs, w_fmix)


def _bias_tiles_kernel(tab_ref, bucket_ref, o_ref):
    h = pl.program_id(1)
    bucket = bucket_ref[0]
    acc = jnp.zeros(bucket.shape, jnp.float32)
    for b in range(REL_BUCKETS):
        acc = jnp.where(bucket == b, tab_ref[b * DIFF_HEADS + h], acc)
    o_ref[0, 0] = acc * LOG2E


def _bias_tiles(rel_bias, bucket_tiles):
    nd, t, _ = bucket_tiles.shape
    return pl.pallas_call(
        _bias_tiles_kernel,
        out_shape=jax.ShapeDtypeStruct((DIFF_HEADS, nd, t, t), jnp.float32),
        grid=(nd, DIFF_HEADS),
        in_specs=[pl.BlockSpec(memory_space=pltpu.SMEM),
                  pl.BlockSpec((1, t, t), lambda d, h: (d, 0, 0))],
        out_specs=pl.BlockSpec((1, 1, t, t), lambda d, h: (h, d, 0, 0)),
        name="bias_tiles",
    )(rel_bias.reshape(-1), bucket_tiles)


def _in_proj_kernel(x_ref, nw_ref, w_ref, pq_ref, qn_ref, kvn_ref, t1_ref,
                    ab_ref, cq_ref, ckv_ref, kr_ref, qd_ref, kd_ref, vd_ref, sg_ref):
    x = x_ref[...]
    xn = _rms(x, nw_ref[...], NORM_EPS).astype(jnp.bfloat16)

    def proj(c0, width):
        return jnp.dot(xn, w_ref[:, c0:c0 + width], preferred_element_type=jnp.float32)

    uf = proj(C_UF, F_WIDTH).astype(jnp.bfloat16)
    c = F_GROUP_DIM
    for g in range(F_GROUPS):
        ab = jnp.dot(uf[:, g * c:(g + 1) * c], pq_ref[g], preferred_element_type=jnp.float32)
        ab_ref[0, 0, :, g * c:(g + 1) * c] = ab[:, :c].astype(jnp.bfloat16)
        ab_ref[0, 1, :, g * c:(g + 1) * c] = ab[:, c:].astype(jnp.bfloat16)

    cq_ref[...] = _rms(proj(C_CQ, Q_LORA), qn_ref[...], NORM_EPS).astype(jnp.bfloat16)
    ckv_ref[...] = _rms(proj(C_CKV, KV_LORA), kvn_ref[...], NORM_EPS).astype(jnp.bfloat16)
    kr_ref[...] = _rope_fold(proj(C_KR, 2 * QK_ROPE), t1_ref[...]).astype(jnp.bfloat16)
    qd_ref[...] = (proj(C_QD, DIFF_WIDTH) * DIFF_QSCALE).astype(jnp.bfloat16)
    kd_ref[...] = proj(C_KD, DIFF_WIDTH).astype(jnp.bfloat16)
    vd_ref[...] = proj(C_VD, DIFF_WIDTH).astype(jnp.bfloat16)
    step = 512
    for c0 in range(0, GATE_WIDTH, step):
        gate = proj(C_GATE + c0, step)
        sg_ref[:, c0:c0 + step] = gate * jax.nn.sigmoid(gate)


def _in_proj(h, norm_w, w_in, pq, q_norm, kv_norm, t1, batch, s_pad):
    rows = h.shape[0]
    tm = SEQ_TILE
    nt = s_pad // tm
    row = lambda width: pl.BlockSpec((tm, width), lambda i: (i, 0))
    const = lambda shape: pl.BlockSpec(shape, lambda i: (0,) * len(shape),
                                       pipeline_mode=pl.Buffered(1))
    bf = jnp.bfloat16
    out_shape = (
        jax.ShapeDtypeStruct((batch, 2, s_pad, F_WIDTH), bf),
        jax.ShapeDtypeStruct((rows, Q_LORA), bf),
        jax.ShapeDtypeStruct((rows, KV_LORA), bf),
        jax.ShapeDtypeStruct((rows, 2 * QK_ROPE), bf),
        jax.ShapeDtypeStruct((rows, DIFF_WIDTH), bf),
        jax.ShapeDtypeStruct((rows, DIFF_WIDTH), bf),
        jax.ShapeDtypeStruct((rows, DIFF_WIDTH), bf),
        jax.ShapeDtypeStruct((rows, GATE_WIDTH), jnp.float32),
    )
    out_specs = (
        pl.BlockSpec((1, 2, tm, F_WIDTH), lambda i: (i // nt, 0, i % nt, 0)),
        row(Q_LORA), row(KV_LORA), row(2 * QK_ROPE),
        row(DIFF_WIDTH), row(DIFF_WIDTH), row(DIFF_WIDTH), row(GATE_WIDTH),
    )
    in_specs = [
        row(D_MODEL),
        const((1, D_MODEL)),
        const((D_MODEL, IN_COLS)),
        const((F_GROUPS, F_GROUP_DIM, 2 * F_GROUP_DIM)),
        const((1, Q_LORA)),
        const((1, KV_LORA)),
        pl.BlockSpec((tm, 2 * QK_ROPE), lambda i: (i % nt, 0)),
    ]
    return pl.pallas_call(
        _in_proj_kernel,
        out_shape=out_shape,
        grid=(rows // tm,),
        in_specs=in_specs,
        out_specs=out_specs,
        compiler_params=pltpu.CompilerParams(
            dimension_semantics=("arbitrary",), vmem_limit_bytes=_vmem_limit(52 << 20)),
        name="in_proj",
    )(h, norm_w, w_in, pq, q_norm, kv_norm, t1)


def _mla_up_kernel(cq_ref, ckv_ref, kr_ref, t1_ref, wq_ref, wkv_ref, q_ref, k_ref, v_ref):
    r = jnp.dot(cq_ref[...], wq_ref[0], preferred_element_type=jnp.float32) * MLA_QSCALE
    q_ref[0, :, :QK_NOPE] = r[:, :QK_NOPE].astype(jnp.bfloat16)
    rope = _rope_fold(r[:, QK_NOPE:], t1_ref[...])
    q_ref[0, :, QK_NOPE:] = rope[:, :QK_ROPE].astype(jnp.bfloat16)
    kv = jnp.dot(ckv_ref[...], wkv_ref[0], preferred_element_type=jnp.float32)
    k_ref[0, :, :QK_NOPE] = kv[:, :QK_NOPE].astype(jnp.bfloat16)
    k_ref[0, :, QK_NOPE:] = kr_ref[:, :QK_ROPE]
    v_ref[0] = kv[:, QK_NOPE:].astype(jnp.bfloat16)


def _mla_up(cq, ckv, kr, t1, wq, wkv, s_pad):
    rows = cq.shape[0]
    tm = SEQ_TILE
    nt = s_pad // tm
    dqk = QK_NOPE + QK_ROPE
    bf = jnp.bfloat16
    return pl.pallas_call(
        _mla_up_kernel,
        out_shape=(jax.ShapeDtypeStruct((MLA_HEADS, rows, dqk), bf),
                   jax.ShapeDtypeStruct((MLA_HEADS, rows, dqk), bf),
                   jax.ShapeDtypeStruct((MLA_HEADS, rows, V_HEAD), bf)),
        grid=(rows // tm, MLA_HEADS),
        in_specs=[pl.BlockSpec((tm, Q_LORA), lambda i, h: (i, 0)),
                  pl.BlockSpec((tm, KV_LORA), lambda i, h: (i, 0)),
                  pl.BlockSpec((tm, 2 * QK_ROPE), lambda i, h: (i, 0)),
                  pl.BlockSpec((tm, 2 * QK_ROPE), lambda i, h: (i % nt, 0)),
                  pl.BlockSpec((1, Q_LORA, QK_NOPE + 2 * QK_ROPE), lambda i, h: (h, 0, 0)),
                  pl.BlockSpec((1, KV_LORA, QK_NOPE + V_HEAD), lambda i, h: (h, 0, 0))],
        out_specs=(pl.BlockSpec((1, tm, dqk), lambda i, h: (h, i, 0)),
                   pl.BlockSpec((1, tm, dqk), lambda i, h: (h, i, 0)),
                   pl.BlockSpec((1, tm, V_HEAD), lambda i, h: (h, i, 0))),
        compiler_params=pltpu.CompilerParams(dimension_semantics=("arbitrary", "arbitrary")),
        name="mla_up",
    )(cq, ckv, kr, t1, wq, wkv)


def _seq_dft_kernel(cs_ref, ab_ref, o_ref):
    n_chunks, _, chunk = cs_ref.shape
    n_half = n_chunks // 2
    o_ref[...] = jnp.zeros(o_ref.shape, jnp.float32)

    def body(c, carry):
        part = c // n_half
        off = pl.multiple_of((c % n_half) * chunk, LANES)
        rhs = ab_ref[0, part, pl.ds(off, chunk), :]
        o_ref[...] += jnp.dot(cs_ref[c], rhs, preferred_element_type=jnp.float32)
        return carry

    lax.fori_loop(0, n_chunks, body, 0)


def _seq_dft(cs, ab, s_pad):
    batch = ab.shape[0]
    tm = SEQ_TILE
    nt = s_pad // tm
    n_chunks, _, chunk = cs.shape
    return pl.pallas_call(
        _seq_dft_kernel,
        out_shape=jax.ShapeDtypeStruct((batch * s_pad, F_WIDTH), jnp.float32),
        grid=(batch, nt),
        in_specs=[pl.BlockSpec((n_chunks, tm, chunk), lambda b, i: (0, i, 0)),
                  pl.BlockSpec((1, 2, s_pad, F_WIDTH), lambda b, i: (b, 0, 0, 0),
                               pipeline_mode=pl.Buffered(1))],
        out_specs=pl.BlockSpec((tm, F_WIDTH), lambda b, i: (b * nt + i, 0)),
        compiler_params=pltpu.CompilerParams(
            dimension_semantics=("arbitrary", "arbitrary"),
            vmem_limit_bytes=_vmem_limit(48 << 20)),
        name="seq_dft",
    )(cs, ab)


def _online_softmax_step(s, v, m_ref, l_ref, acc_ref):
    m_old = m_ref[...]
    m_new = jnp.maximum(m_old, jnp.max(s, axis=-1, keepdims=True))
    p = jnp.exp2(s - m_new)
    alpha = jnp.exp2(m_old - m_new)
    l_ref[...] = alpha * l_ref[...] + jnp.sum(p, axis=-1, keepdims=True)
    acc_ref[...] = alpha * acc_ref[...] + jnp.dot(
        p.astype(jnp.bfloat16), v, preferred_element_type=jnp.float32)
    m_ref[...] = m_new


def _mask_tail(s, n_valid):
    col = lax.broadcasted_iota(jnp.int32, s.shape, 1)
    return jnp.where(col < n_valid, s, MASK_VALUE)


def _mla_attn_kernel(q_ref, k_ref, v_ref, o_ref, m_ref, l_ref, acc_ref, *, seq_len, tk):
    s_pad = k_ref.shape[1]
    nk = s_pad // tk
    q = q_ref[0]
    m_ref[...] = jnp.full(m_ref.shape, MASK_VALUE, jnp.float32)
    l_ref[...] = jnp.zeros(l_ref.shape, jnp.float32)
    acc_ref[...] = jnp.zeros(acc_ref.shape, jnp.float32)

    def chunk(j, n_valid):
        off = pl.multiple_of(j * tk, tk)
        k = k_ref[0, pl.ds(off, tk), :]
        v = v_ref[0, pl.ds(off, tk), :]
        s = lax.dot_general(q, k, (((1,), (1,)), ((), ())), preferred_element_type=jnp.float32)
        if n_valid < tk:
            s = _mask_tail(s, n_valid)
        _online_softmax_step(s, v, m_ref, l_ref, acc_ref)

    def body(j, carry):
        chunk(j, tk)
        return carry

    lax.fori_loop(0, nk - 1, body, 0)
    chunk(nk - 1, seq_len - (nk - 1) * tk)
    o_ref[...] = (acc_ref[...] / l_ref[...]).astype(o_ref.dtype)


def _mla_attn(q, k, v, batch, s_pad, seq_len):
    tq = tk = ATT_TILE
    nq = s_pad // tq
    dqk = QK_NOPE + QK_ROPE
    return pl.pallas_call(
        functools.partial(_mla_attn_kernel, seq_len=seq_len, tk=tk),
        out_shape=jax.ShapeDtypeStruct((batch * s_pad, MLA_WIDTH), jnp.bfloat16),
        grid=(batch, MLA_HEADS, nq),
        in_specs=[pl.BlockSpec((1, tq, dqk), lambda b, h, i: (h, b * nq + i, 0)),
                  pl.BlockSpec((1, s_pad, dqk), lambda b, h, i: (h, b, 0)),
                  pl.BlockSpec((1, s_pad, V_HEAD), lambda b, h, i: (h, b, 0))],
        out_specs=pl.BlockSpec((tq, V_HEAD), lambda b, h, i: (b * nq + i, h)),
        scratch_shapes=[pltpu.VMEM((tq, 1), jnp.float32),
                        pltpu.VMEM((tq, 1), jnp.float32),
                        pltpu.VMEM((tq, V_HEAD), jnp.float32)],
        compiler_params=pltpu.CompilerParams(
            dimension_semantics=("arbitrary", "arbitrary", "arbitrary"),
            vmem_limit_bytes=_vmem_limit(40 << 20)),
        name="mla_attn",
    )(q, k, v)


def _diff_attn_kernel(q_ref, k_ref, v_ref, bt_ref, lam_ref, g_ref, o_ref,
                      m_ref, l_ref, acc_ref, *, seq_len, tk, lam_init):
    s_pad = k_ref.shape[0]
    nk = s_pad // tk
    tq = q_ref.shape[0]
    i = pl.program_id(2)
    qq = q_ref[...].astype(jnp.float32)
    lane = lax.broadcasted_iota(jnp.int32, qq.shape, 1)
    zero = jnp.zeros_like(qq)
    q2x = jnp.concatenate([jnp.where(lane < DIFF_QK, qq, zero),
                           jnp.where(lane >= DIFF_QK, qq, zero)], axis=0).astype(jnp.bfloat16)
    m_ref[...] = jnp.full(m_ref.shape, MASK_VALUE, jnp.float32)
    l_ref[...] = jnp.zeros(l_ref.shape, jnp.float32)
    acc_ref[...] = jnp.zeros(acc_ref.shape, jnp.float32)
    n_side = (bt_ref.shape[1] - 1) // 2

    def chunk(j, n_valid):
        off = pl.multiple_of(j * tk, tk)
        k = k_ref[pl.ds(off, tk), :]
        v = v_ref[pl.ds(off, tk), :]
        s = lax.dot_general(q2x, k, (((1,), (1,)), ((), ())), preferred_element_type=jnp.float32)
        bias = bt_ref[0, jnp.clip(j - i, -n_side, n_side) + n_side]
        s = s + jnp.concatenate([bias, bias], axis=0)
        if n_valid < tk:
            s = _mask_tail(s, n_valid)
        _online_softmax_step(s, v, m_ref, l_ref, acc_ref)

    def body(j, carry):
        chunk(j, tk)
        return carry

    lax.fori_loop(0, nk - 1, body, 0)
    chunk(nk - 1, seq_len - (nk - 1) * tk)

    o = acc_ref[...] / l_ref[...]
    lv = lam_ref[...]
    lam = (jnp.exp(jnp.sum(lv[0:1] * lv[1:2], axis=-1, keepdims=True))
           - jnp.exp(jnp.sum(lv[2:3] * lv[3:4], axis=-1, keepdims=True)) + lam_init)
    od = o[:tq] - lam * o[tq:]
    o_ref[...] = (_rms(od, g_ref[...], DIFF_NORM_EPS) * (1.0 - lam_init)).astype(o_ref.dtype)


def _diff_attn(qd, kd, vd, bias_tiles, lam_vec, diff_norm, batch, s_pad, seq_len, lam_init):
    tq = tk = ATT_TILE
    nq = s_pad // tq
    nd = bias_tiles.shape[1]
    return pl.pallas_call(
        functools.partial(_diff_attn_kernel, seq_len=seq_len, tk=tk, lam_init=lam_init),
        out_shape=jax.ShapeDtypeStruct((batch * s_pad, DIFF_WIDTH), jnp.bfloat16),
        grid=(batch, DIFF_HEADS, nq),
        in_specs=[pl.BlockSpec((tq, DIFF_V), lambda b, h, i: (b * nq + i, h)),
                  pl.BlockSpec((s_pad, DIFF_V), lambda b, h, i: (b, h)),
                  pl.BlockSpec((s_pad, DIFF_V), lambda b, h, i: (b, h)),
                  pl.BlockSpec((1, nd, tq, tk), lambda b, h, i: (h, 0, 0, 0)),
                  pl.BlockSpec((4, DIFF_QK), lambda b, h, i: (0, 0)),
                  pl.BlockSpec((1, DIFF_V), lambda b, h, i: (0, 0))],
        out_specs=pl.BlockSpec((tq, DIFF_V), lambda b, h, i: (b * nq + i, h)),
        scratch_shapes=[pltpu.VMEM((2 * tq, 1), jnp.float32),
                        pltpu.VMEM((2 * tq, 1), jnp.float32),
                        pltpu.VMEM((2 * tq, DIFF_V), jnp.float32)],
        compiler_params=pltpu.CompilerParams(
            dimension_semantics=("arbitrary", "arbitrary", "arbitrary"),
            vmem_limit_bytes=_vmem_limit(40 << 20)),
        name="diff_attn",
    )(qd, kd, vd, bias_tiles, lam_vec, diff_norm)


def _out_proj_kernel(x_ref, yf_ref, ym_ref, yd_ref, sg_ref, wo_ref, fn_ref, o_ref, *, final):
    c1 = F_WIDTH
    c2 = F_WIDTH + MLA_WIDTH

    def seg(y, c0, c1_):
        yg = (y.astype(jnp.float32) * sg_ref[:, c0:c1_]).astype(jnp.bfloat16)
        return jnp.dot(yg, wo_ref[c0:c1_, :], preferred_element_type=jnp.float32)

    o = (x_ref[...] + seg(yf_ref[...], 0, c1) + seg(ym_ref[...], c1, c2)
         + seg(yd_ref[...], c2, D_MODEL))
    if final:
        o = _rms(o, fn_ref[...], NORM_EPS)
    o_ref[...] = o


def _out_proj(h, yf, ym, yd, sg, wo, final_norm, final):
    rows = h.shape[0]
    tm = SEQ_TILE
    row = lambda width: pl.BlockSpec((tm, width), lambda i: (i, 0))
    const = lambda shape: pl.BlockSpec(shape, lambda i: (0,) * len(shape),
                                       pipeline_mode=pl.Buffered(1))
    return pl.pallas_call(
        functools.partial(_out_proj_kernel, final=final),
        out_shape=jax.ShapeDtypeStruct((rows, D_MODEL), jnp.float32),
        grid=(rows // tm,),
        in_specs=[row(D_MODEL), row(F_WIDTH), row(MLA_WIDTH), row(DIFF_WIDTH), row(GATE_WIDTH),
                  const((D_MODEL, D_MODEL)), const((1, D_MODEL))],
        out_specs=row(D_MODEL),
        compiler_params=pltpu.CompilerParams(
            dimension_semantics=("arbitrary",), vmem_limit_bytes=_vmem_limit(40 << 20)),
        name="out_proj",
    )(h, yf, ym, yd, sg, wo, final_norm)


def _t5_bucket(rel):
    nb = REL_BUCKETS // 2
    max_exact = nb // 2
    ret = (rel > 0).astype(jnp.int32) * nb
    n = jnp.abs(rel)
    nf = jnp.maximum(n, 1).astype(jnp.float32)
    large = max_exact + (jnp.log(nf / max_exact) / math.log(REL_MAX_DIST / max_exact)
                         * (nb - max_exact)).astype(jnp.int32)
    large = jnp.minimum(large, nb - 1)
    return ret + jnp.where(n < max_exact, n, large)


def _bucket_tiles(t):
    n_side = 2
    assert t >= REL_MAX_DIST
    d = jnp.arange(-n_side, n_side + 1, dtype=jnp.int32)[:, None, None] * t
    row = jnp.arange(t, dtype=jnp.int32)[None, :, None]
    col = jnp.arange(t, dtype=jnp.int32)[None, None, :]
    return _t5_bucket(d + col - row)


def _rope_table(s_pad):
    pos = jnp.arange(s_pad, dtype=jnp.float32)
    inv_freq = ROPE_THETA ** (-jnp.arange(0, QK_ROPE, 2, dtype=jnp.float32) / QK_ROPE)
    ang = pos[:, None] * inv_freq[None, :]
    cos, sin = jnp.cos(ang), jnp.sin(ang)
    return jnp.concatenate([cos, cos, -sin, sin], axis=-1)


def _dft_angle_tables(n, rows, cols):
    r = (jnp.arange(rows, dtype=jnp.int32)[:, None] * jnp.arange(cols, dtype=jnp.int32)[None, :]) % n
    ang = r.astype(jnp.float32) * (2.0 * math.pi / n)
    return jnp.cos(ang), jnp.sin(ang)


def _seq_dft_matrix(seq_len, s_pad):
    t = SEQ_TILE
    na = s_pad // t
    k = jnp.arange(s_pad, dtype=jnp.int32)
    ra = (jnp.arange(na, dtype=jnp.int32)[:, None] * t * k[None, :]) % seq_len
    rb = (jnp.arange(t, dtype=jnp.int32)[:, None] * k[None, :]) % seq_len
    w = 2.0 * math.pi / seq_len
    aa = ra.astype(jnp.float32) * w
    bb = rb.astype(jnp.float32) * w
    ca, sa = jnp.cos(aa)[:, None, :], jnp.sin(aa)[:, None, :]
    cb, sb = jnp.cos(bb)[None], jnp.sin(bb)[None]
    scale = 1.0 / math.sqrt(seq_len)
    cosm = ((ca * cb - sa * sb) * scale).reshape(s_pad, s_pad)
    sinm = ((sa * cb + ca * sb) * scale).reshape(s_pad, s_pad)
    valid = (k[:, None] < seq_len) & (k[None, :] < seq_len)
    zero = jnp.zeros((), jnp.float32)
    cosm = jnp.where(valid, cosm, zero).astype(jnp.bfloat16)
    sinm = jnp.where(valid, sinm, zero).astype(jnp.bfloat16)
    chunk = _pick_dft_chunk(s_pad)
    both = jnp.stack([cosm, sinm], axis=0).reshape(2, s_pad, s_pad // chunk, chunk)
    return both.transpose(0, 2, 1, 3).reshape(2 * s_pad // chunk, s_pad, chunk)


def _layer_weights(l, w_in, w_uq, w_ukv, w_o):
    bf = jnp.bfloat16
    w = w_in[l]
    s = np.cumsum([0, F_WIDTH, Q_LORA, KV_LORA, QK_ROPE, DIFF_WIDTH, DIFF_WIDTH, DIFF_WIDTH,
                   GATE_WIDTH])
    uf, cq, ckv, kr, qd, kd, vd, gate = (w[:, s[i]:s[i + 1]] for i in range(8))
    half = QK_ROPE // 2
    kr_sw = jnp.concatenate([kr[:, half:], kr[:, :half]], axis=1)
    w_in_r = jnp.concatenate([uf, cq, ckv, kr, kr_sw, qd, kd, vd, gate], axis=1).astype(bf)
    wq = w_uq[l].reshape(Q_LORA, MLA_HEADS, QK_NOPE + QK_ROPE)
    rope = wq[..., QK_NOPE:]
    rope_sw = jnp.concatenate([rope[..., half:], rope[..., :half]], axis=-1)
    wq = jnp.concatenate([wq, rope_sw], axis=-1).transpose(1, 0, 2).astype(bf)
    wkv = w_ukv[l].reshape(KV_LORA, MLA_HEADS, QK_NOPE + V_HEAD).transpose(1, 0, 2).astype(bf)
    return w_in_r, wq, wkv, w_o[l].astype(bf)


def _pick_dft_chunk(s_pad):
    for c in range(min(s_pad, DFT_MAX_CHUNK), 0, -LANES):
        if s_pad % c == 0:
            return c
    return s_pad


def _encode(x, meta_tokens, bias_tiles, final_norm, norm_w, q_norm, kv_norm, diff_norm,
            lam_vecs, pq, layer_w):
    batch, s_real, _ = x.shape
    seq_len = s_real + N_META
    s_pad = -(-seq_len // SEQ_TILE) * SEQ_TILE
    rows = batch * s_pad
    meta = jnp.broadcast_to(meta_tokens[None], (batch, N_META, D_MODEL))
    pad = jnp.zeros((batch, s_pad - seq_len, D_MODEL), x.dtype)
    h = jnp.concatenate([meta, x, pad], axis=1).reshape(rows, D_MODEL)

    t1 = _rope_table(s_pad)
    cs = _seq_dft_matrix(seq_len, s_pad)
    for l in range(DEPTH):
        w_in_r, wq, wkv, wo = layer_w[l]
        lam_init = 0.8 - 0.6 * math.exp(-0.3 * l)
        ab, cq, ckv, kr, qd, kd, vd, sg = _in_proj(
            h, norm_w[l][None], w_in_r, pq[l], q_norm[l][None], kv_norm[l][None], t1, batch, s_pad)
        q, k, v = _mla_up(cq, ckv, kr, t1, wq, wkv, s_pad)
        yf = _seq_dft(cs, ab, s_pad)
        ym = _mla_attn(q, k, v, batch, s_pad, seq_len)
        yd = _diff_attn(qd, kd, vd, bias_tiles, lam_vecs[l], diff_norm[l][None],
                        batch, s_pad, seq_len, lam_init)
        h = _out_proj(h, yf, ym, yd, sg, wo, final_norm[None], final=(l == DEPTH - 1))
    return h.reshape(batch, s_pad, D_MODEL)[:, N_META:seq_len]


def kernel(x_prompt, x_sample, meta_tokens, rel_bias, final_norm, norm_w, w_in, w_fmix, q_norm,
           w_uq, kv_norm, w_ukv, lam_q1, lam_k1, lam_q2, lam_k2, diff_norm, w_o):
    cc, sc = _dft_angle_tables(F_GROUP_DIM, F_GROUP_DIM, F_GROUP_DIM)
    chan_cs = jnp.stack([cc, -sc]) * (1.0 / math.sqrt(F_GROUP_DIM))
    pq = _fold_fmix(chan_cs, w_fmix).astype(jnp.bfloat16)
    bias_tiles = _bias_tiles(rel_bias, _bucket_tiles(ATT_TILE))
    lam_vecs = jnp.stack([lam_q1, lam_k1, lam_q2, lam_k2], axis=1)
    layer_w = [_layer_weights(l, w_in, w_uq, w_ukv, w_o) for l in range(DEPTH)]
    enc = functools.partial(
        _encode, meta_tokens=meta_tokens, bias_tiles=bias_tiles, final_norm=final_norm,
        norm_w=norm_w, q_norm=q_norm, kv_norm=kv_norm, diff_norm=diff_norm,
        lam_vecs=lam_vecs, pq=pq, layer_w=layer_w)
    return (enc(x_prompt), enc(x_sample))
```

```python
import functools
import math
from typing import NamedTuple

import jax
import jax.numpy as jnp
import numpy as np
from jax import lax
from jax.experimental import pallas as pl
from jax.experimental.pallas import tpu as pltpu

D_MODEL = 2048
DEPTH = 2
N_META = 16
F_GROUPS = 4
F_GROUP_DIM = 128
F_WIDTH = F_GROUPS * F_GROUP_DIM
MLA_HEADS = 8
Q_LORA = 768
KV_LORA = 512
QK_NOPE = 128
QK_ROPE = 64
V_HEAD = 128
MLA_WIDTH = MLA_HEADS * V_HEAD
ROPE_THETA = 10000.0
DIFF_HEADS = 4
DIFF_QK = 64
DIFF_V = 2 * DIFF_QK
DIFF_WIDTH = DIFF_HEADS * DIFF_V
REL_BUCKETS = 32
REL_MAX_DIST = 128
NORM_EPS = 1e-6
DIFF_NORM_EPS = 1e-5
GATE_WIDTH = D_MODEL

LOG2E = math.log2(math.e)
MLA_QSCALE = LOG2E / math.sqrt(QK_NOPE + QK_ROPE)
DIFF_QSCALE = LOG2E / math.sqrt(DIFF_QK)
MASK_VALUE = -1e30

LANES = 128
VMEM_BYTES_V7X = 64 * 1024 * 1024

C_UF = 0
C_CQ = C_UF + F_WIDTH
C_CKV = C_CQ + Q_LORA
C_KR = C_CKV + KV_LORA
C_QD = C_KR + 2 * QK_ROPE
C_KD = C_QD + DIFF_WIDTH
C_VD = C_KD + DIFF_WIDTH
C_GATE = C_VD + DIFF_WIDTH
IN_COLS = C_GATE + GATE_WIDTH

MAX_ROW_TILE = 384
MAX_KEY_CHUNK = 1408
BIAS_SIDE = 2


class SeqPlan(NamedTuple):
    batch: int
    seq_len: int
    s_pad: int
    tile: int
    chunk: int


def _round_up(n, m):
    return -(-n // m) * m


def _largest_tile(s_pad, limit):
    n = s_pad // LANES
    return LANES * max(d for d in range(1, n + 1) if n % d == 0 and d * LANES <= limit)


def _plan(batch, s_real):
    seq_len = s_real + N_META
    tile = min(MAX_ROW_TILE, _round_up(seq_len, LANES))
    s_pad = _round_up(seq_len, tile)
    return SeqPlan(batch, seq_len, s_pad, tile, _largest_tile(s_pad, MAX_KEY_CHUNK))


def _vmem_limit(nbytes):
    return int(min(nbytes, VMEM_BYTES_V7X - (8 << 20)))


def _rms(x, g, eps):
    ms = jnp.mean(x * x, axis=-1, keepdims=True)
    return x * lax.rsqrt(ms + eps) * g


def _rope_fold(v, t1):
    w = v * t1
    return w + pltpu.roll(w, QK_ROPE, 1)


def _fold_fmix_kernel(cs_ref, w_ref, o_ref):
    w = w_ref[0, 0]
    o_ref[0, 0, :, :F_GROUP_DIM] = jnp.dot(
        cs_ref[0], w, preferred_element_type=jnp.float32,
        precision=lax.Precision.HIGHEST)
    o_ref[0, 0, :, F_GROUP_DIM:] = jnp.dot(
        cs_ref[1], w, preferred_element_type=jnp.float32,
        precision=lax.Precision.HIGHEST)


def _fold_fmix(chan_cs, w_fmix):
    c = F_GROUP_DIM
    return pl.pallas_call(
        _fold_fmix_kernel,
        out_shape=jax.ShapeDtypeStruct((DEPTH, F_GROUPS, c, 2 * c), jnp.float32),
        grid=(DEPTH, F_GROUPS),
        in_specs=[pl.BlockSpec((2, c, c), lambda l, g: (0, 0, 0)),
                  pl.BlockSpec((1, 1, c, c), lambda l, g: (l, g, 0, 0))],
        out_specs=pl.BlockSpec((1, 1, c, 2 * c), lambda l, g: (l, g, 0, 0)),
        name="fold_fmix",
    )(chan_cs, w_fmix)


def _bias_tiles_kernel(tab_ref, bucket_ref, o_ref):
    h = pl.program_id(1)
    bucket = bucket_ref[0]
    acc = jnp.zeros(bucket.shape, jnp.float32)
    for b in range(REL_BUCKETS):
        acc = jnp.where(bucket == b, tab_ref[b * DIFF_HEADS + h], acc)
    o_ref[0, 0] = acc * LOG2E


def _bias_tiles(rel_bias, bucket_tiles):
    nd, t, w = bucket_tiles.shape
    return pl.pallas_call(
        _bias_tiles_kernel,
        out_shape=jax.ShapeDtypeStruct((DIFF_HEADS, nd, t, w), jnp.float32),
        grid=(nd, DIFF_HEADS),
        in_specs=[pl.BlockSpec(memory_space=pltpu.SMEM),
                  pl.BlockSpec((1, t, w), lambda d, h: (d, 0, 0))],
        out_specs=pl.BlockSpec((1, 1, t, w), lambda d, h: (h, d, 0, 0)),
        name="bias_tiles",
    )(rel_bias.reshape(-1), bucket_tiles)


def _in_proj_kernel(x_ref, nw_ref, w_ref, pq_ref, qn_ref, kvn_ref, t1_ref,
                    ab_ref, cq_ref, ckv_ref, kr_ref, qd_ref, kd_ref, vd_ref, sg_ref):
    x = x_ref[...]
    xn = _rms(x, nw_ref[...], NORM_EPS).astype(jnp.bfloat16)

    def proj(c0, width):
        return jnp.dot(xn, w_ref[:, c0:c0 + width], preferred_element_type=jnp.float32)

    uf = proj(C_UF, F_WIDTH).astype(jnp.bfloat16)
    c = F_GROUP_DIM
    for g in range(F_GROUPS):
        ab = jnp.dot(uf[:, g * c:(g + 1) * c], pq_ref[g], preferred_element_type=jnp.float32)
        ab_ref[0, 0, :, g * c:(g + 1) * c] = ab[:, :c].astype(jnp.bfloat16)
        ab_ref[0, 1, :, g * c:(g + 1) * c] = ab[:, c:].astype(jnp.bfloat16)

    cq_ref[...] = _rms(proj(C_CQ, Q_LORA), qn_ref[...], NORM_EPS).astype(jnp.bfloat16)
    ckv_ref[...] = _rms(proj(C_CKV, KV_LORA), kvn_ref[...], NORM_EPS).astype(jnp.bfloat16)
    kr_ref[...] = _rope_fold(proj(C_KR, 2 * QK_ROPE), t1_ref[...]).astype(jnp.bfloat16)
    qd_ref[...] = (proj(C_QD, DIFF_WIDTH) * DIFF_QSCALE).astype(jnp.bfloat16)
    kd_ref[...] = proj(C_KD, DIFF_WIDTH).astype(jnp.bfloat16)
    vd_ref[...] = proj(C_VD, DIFF_WIDTH).astype(jnp.bfloat16)
    step = 512
    for c0 in range(0, GATE_WIDTH, step):
        gate = proj(C_GATE + c0, step)
        sg_ref[:, c0:c0 + step] = gate * jax.nn.sigmoid(gate)


def _in_proj(h, norm_w, w_in, pq, q_norm, kv_norm, t1, plan):
    rows = h.shape[0]
    tm = plan.tile
    nt = plan.s_pad // tm
    row = lambda width: pl.BlockSpec((tm, width), lambda i: (i, 0))
    const = lambda shape: pl.BlockSpec(shape, lambda i: (0,) * len(shape),
                                       pipeline_mode=pl.Buffered(1))
    bf = jnp.bfloat16
    out_shape = (
        jax.ShapeDtypeStruct((plan.batch, 2, plan.s_pad, F_WIDTH), bf),
        jax.ShapeDtypeStruct((rows, Q_LORA), bf),
        jax.ShapeDtypeStruct((rows, KV_LORA), bf),
        jax.ShapeDtypeStruct((rows, 2 * QK_ROPE), bf),
        jax.ShapeDtypeStruct((rows, DIFF_WIDTH), bf),
        jax.ShapeDtypeStruct((rows, DIFF_WIDTH), bf),
        jax.ShapeDtypeStruct((rows, DIFF_WIDTH), bf),
        jax.ShapeDtypeStruct((rows, GATE_WIDTH), jnp.float32),
    )
    out_specs = (
        pl.BlockSpec((1, 2, tm, F_WIDTH), lambda i: (i // nt, 0, i % nt, 0)),
        row(Q_LORA), row(KV_LORA), row(2 * QK_ROPE),
        row(DIFF_WIDTH), row(DIFF_WIDTH), row(DIFF_WIDTH), row(GATE_WIDTH),
    )
    in_specs = [
        row(D_MODEL),
        const((1, D_MODEL)),
        const((D_MODEL, IN_COLS)),
        const((F_GROUPS, F_GROUP_DIM, 2 * F_GROUP_DIM)),
        const((1, Q_LORA)),
        const((1, KV_LORA)),
        pl.BlockSpec((tm, 2 * QK_ROPE), lambda i: (i % nt, 0)),
    ]
    return pl.pallas_call(
        _in_proj_kernel,
        out_shape=out_shape,
        grid=(rows // tm,),
        in_specs=in_specs,
        out_specs=out_specs,
        compiler_params=pltpu.CompilerParams(
            dimension_semantics=("arbitrary",), vmem_limit_bytes=_vmem_limit(56 << 20)),
        name="in_proj",
    )(h, norm_w, w_in, pq, q_norm, kv_norm, t1)


def _mla_up_kernel(cq_ref, ckv_ref, kr_ref, t1_ref, wq_ref, wkv_ref, q_ref, k_ref, v_ref):
    cq = cq_ref[...]
    ckv = ckv_ref[...]
    t1 = t1_ref[...]
    k_rope = kr_ref[:, :QK_ROPE]
    for h in range(MLA_HEADS):
        r = jnp.dot(cq, wq_ref[h], preferred_element_type=jnp.float32) * MLA_QSCALE
        q_ref[h, :, :QK_NOPE] = r[:, :QK_NOPE].astype(jnp.bfloat16)
        rope = _rope_fold(r[:, QK_NOPE:], t1)
        q_ref[h, :, QK_NOPE:] = rope[:, :QK_ROPE].astype(jnp.bfloat16)
        kv = jnp.dot(ckv, wkv_ref[h], preferred_element_type=jnp.float32)
        k_ref[h, :, :QK_NOPE] = kv[:, :QK_NOPE].astype(jnp.bfloat16)
        k_ref[h, :, QK_NOPE:] = k_rope
        v_ref[h] = kv[:, QK_NOPE:].astype(jnp.bfloat16)


def _mla_up(cq, ckv, kr, t1, wq, wkv, plan):
    rows = cq.shape[0]
    tm = plan.tile
    nt = plan.s_pad // tm
    dqk = QK_NOPE + QK_ROPE
    bf = jnp.bfloat16
    row = lambda width: pl.BlockSpec((tm, width), lambda i: (i, 0))
    heads = lambda width: pl.BlockSpec((MLA_HEADS, tm, width), lambda i: (0, i, 0))
    const = lambda shape: pl.BlockSpec(shape, lambda i: (0,) * len(shape))
    return pl.pallas_call(
        _mla_up_kernel,
        out_shape=(jax.ShapeDtypeStruct((MLA_HEADS, rows, dqk), bf),
                   jax.ShapeDtypeStruct((MLA_HEADS, rows, dqk), bf),
                   jax.ShapeDtypeStruct((MLA_HEADS, rows, V_HEAD), bf)),
        grid=(rows // tm,),
        in_specs=[row(Q_LORA), row(KV_LORA), row(2 * QK_ROPE),
                  pl.BlockSpec((tm, 2 * QK_ROPE), lambda i: (i % nt, 0)),
                  const((MLA_HEADS, Q_LORA, QK_NOPE + 2 * QK_ROPE)),
                  const((MLA_HEADS, KV_LORA, QK_NOPE + V_HEAD))],
        out_specs=(heads(dqk), heads(dqk), heads(V_HEAD)),
        compiler_params=pltpu.CompilerParams(dimension_semantics=("arbitrary",)),
        name="mla_up",
    )(cq, ckv, kr, t1, wq, wkv)


def _seq_dft_kernel(cs_ref, ab_ref, o_ref):
    n_chunks, _, chunk = cs_ref.shape
    n_half = n_chunks // 2
    o_ref[...] = jnp.zeros(o_ref.shape, jnp.float32)

    def body(c, carry):
        part = c // n_half
        off = pl.multiple_of((c % n_half) * chunk, LANES)
        rhs = ab_ref[0, part, pl.ds(off, chunk), :]
        o_ref[...] += jnp.dot(cs_ref[c], rhs, preferred_element_type=jnp.float32)
        return carry

    lax.fori_loop(0, n_chunks, body, 0)


def _seq_dft(cs, ab, plan):
    tm = plan.tile
    nt = plan.s_pad // tm
    n_chunks, _, chunk = cs.shape
    return pl.pallas_call(
        _seq_dft_kernel,
        out_shape=jax.ShapeDtypeStruct((plan.batch * plan.s_pad, F_WIDTH), jnp.float32),
        grid=(plan.batch, nt),
        in_specs=[pl.BlockSpec((n_chunks, tm, chunk), lambda b, i: (0, i, 0)),
                  pl.BlockSpec((1, 2, plan.s_pad, F_WIDTH), lambda b, i: (b, 0, 0, 0),
                               pipeline_mode=pl.Buffered(1))],
        out_specs=pl.BlockSpec((tm, F_WIDTH), lambda b, i: (b * nt + i, 0)),
        compiler_params=pltpu.CompilerParams(
            dimension_semantics=("arbitrary", "arbitrary"),
            vmem_limit_bytes=_vmem_limit(56 << 20)),
        name="seq_dft",
    )(cs, ab)


def _key_chunks(plan):
    k_end = _round_up(plan.seq_len, LANES)
    return [(c0, min(plan.chunk, k_end - c0)) for c0 in range(0, k_end, plan.chunk)]


def _mask_pad_keys(s, c0, seq_len):
    width = s.shape[-1]
    n_valid = seq_len - c0
    if n_valid >= width:
        return s
    a = width - LANES
    assert n_valid > a
    lane = lax.broadcasted_iota(jnp.int32, (s.shape[0], LANES), 1)
    tail = jnp.where(lane < n_valid - a, s[:, a:], MASK_VALUE)
    return jnp.concatenate([s[:, :a], tail], axis=1) if a else tail


def _softmax_step(s, v, state):
    if state is None:
        m_new = jnp.max(s, axis=-1, keepdims=True)
        p = jnp.exp2(s - m_new)
        l = jnp.sum(p, axis=-1, keepdims=True)
        acc = jnp.dot(p.astype(jnp.bfloat16), v, preferred_element_type=jnp.float32)
        return m_new, l, acc
    m, l, acc = state
    m_new = jnp.maximum(m, jnp.max(s, axis=-1, keepdims=True))
    p = jnp.exp2(s - m_new)
    alpha = jnp.exp2(m - m_new)
    l = alpha * l + jnp.sum(p, axis=-1, keepdims=True)
    acc = alpha * acc + jnp.dot(p.astype(jnp.bfloat16), v, preferred_element_type=jnp.float32)
    return m_new, l, acc


def _qk(q, k):
    return lax.dot_general(q, k, (((1,), (1,)), ((), ())), preferred_element_type=jnp.float32)


def _mla_attn_kernel(q_ref, k_ref, v_ref, o_ref, *, chunks, seq_len):
    q = q_ref[0]
    state = None
    for c0, width in chunks:
        s = _qk(q, k_ref[0, c0:c0 + width, :])
        s = _mask_pad_keys(s, c0, seq_len)
        state = _softmax_step(s, v_ref[0, c0:c0 + width, :], state)
    _, l, acc = state
    o_ref[...] = (acc / l).astype(o_ref.dtype)


def _mla_attn(q, k, v, plan):
    tq = plan.tile
    nq = plan.s_pad // tq
    dqk = QK_NOPE + QK_ROPE
    return pl.pallas_call(
        functools.partial(_mla_attn_kernel, chunks=_key_chunks(plan), seq_len=plan.seq_len),
        out_shape=jax.ShapeDtypeStruct((plan.batch * plan.s_pad, MLA_WIDTH), jnp.bfloat16),
        grid=(plan.batch, MLA_HEADS, nq),
        in_specs=[pl.BlockSpec((1, tq, dqk), lambda b, h, i: (h, b * nq + i, 0)),
                  pl.BlockSpec((1, plan.s_pad, dqk), lambda b, h, i: (h, b, 0)),
                  pl.BlockSpec((1, plan.s_pad, V_HEAD), lambda b, h, i: (h, b, 0))],
        out_specs=pl.BlockSpec((tq, V_HEAD), lambda b, h, i: (b * nq + i, h)),
        compiler_params=pltpu.CompilerParams(
            dimension_semantics=("arbitrary", "arbitrary", "arbitrary"),
            vmem_limit_bytes=_vmem_limit(48 << 20)),
        name="mla_attn",
    )(q, k, v)


def _diff_attn_kernel(q_ref, k_ref, v_ref, bt_ref, lam_ref, g_ref, o_ref,
                      *, chunks, seq_len, lam_init):
    tq = q_ref.shape[0]
    q_tiles = tq // LANES
    i = pl.program_id(2)
    qq = q_ref[...].astype(jnp.float32)
    lane = lax.broadcasted_iota(jnp.int32, qq.shape, 1)
    zero = jnp.zeros_like(qq)
    q2x = jnp.concatenate([jnp.where(lane < DIFF_QK, qq, zero),
                           jnp.where(lane >= DIFF_QK, qq, zero)], axis=0).astype(jnp.bfloat16)
    state = None
    for c0, width in chunks:
        s = _qk(q2x, k_ref[c0:c0 + width, :])
        parts = []
        for g in range(width // LANES):
            o = (c0 // LANES + g) - i * q_tiles
            bias = bt_ref[0, jnp.clip(o, -BIAS_SIDE, q_tiles + BIAS_SIDE - 1) + BIAS_SIDE]
            sub = s[:, g * LANES:(g + 1) * LANES]
            parts.append(jnp.concatenate([sub[:tq] + bias, sub[tq:] + bias], axis=0))
        s = jnp.concatenate(parts, axis=1)
        s = _mask_pad_keys(s, c0, seq_len)
        state = _softmax_step(s, v_ref[c0:c0 + width, :], state)
    _, l, acc = state
    o = acc / l
    lv = lam_ref[...]
    lam = (jnp.exp(jnp.sum(lv[0:1] * lv[1:2], axis=-1, keepdims=True))
           - jnp.exp(jnp.sum(lv[2:3] * lv[3:4], axis=-1, keepdims=True)) + lam_init)
    od = o[:tq] - lam * o[tq:]
    o_ref[...] = (_rms(od, g_ref[...], DIFF_NORM_EPS) * (1.0 - lam_init)).astype(o_ref.dtype)


def _diff_attn(qd, kd, vd, bias_tiles, lam_vec, diff_norm, plan, lam_init):
    tq = plan.tile
    nq = plan.s_pad // tq
    nd = bias_tiles.shape[1]
    return pl.pallas_call(
        functools.partial(_diff_attn_kernel, chunks=_key_chunks(plan), seq_len=plan.seq_len,
                          lam_init=lam_init),
        out_shape=jax.ShapeDtypeStruct((plan.batch * plan.s_pad, DIFF_WIDTH), jnp.bfloat16),
        grid=(plan.batch, DIFF_HEADS, nq),
        in_specs=[pl.BlockSpec((tq, DIFF_V), lambda b, h, i: (b * nq + i, h)),
                  pl.BlockSpec((plan.s_pad, DIFF_V), lambda b, h, i: (b, h)),
                  pl.BlockSpec((plan.s_pad, DIFF_V), lambda b, h, i: (b, h)),
                  pl.BlockSpec((1, nd, tq, LANES), lambda b, h, i: (h, 0, 0, 0)),
                  pl.BlockSpec((4, DIFF_QK), lambda b, h, i: (0, 0)),
                  pl.BlockSpec((1, DIFF_V), lambda b, h, i: (0, 0))],
        out_specs=pl.BlockSpec((tq, DIFF_V), lambda b, h, i: (b * nq + i, h)),
        compiler_params=pltpu.CompilerParams(
            dimension_semantics=("arbitrary", "arbitrary", "arbitrary"),
            vmem_limit_bytes=_vmem_limit(48 << 20)),
        name="diff_attn",
    )(qd, kd, vd, bias_tiles, lam_vec, diff_norm)


def _out_proj_kernel(x_ref, yf_ref, ym_ref, yd_ref, sg_ref, wo_ref, fn_ref, o_ref, *, final):
    c1 = F_WIDTH
    c2 = F_WIDTH + MLA_WIDTH

    def seg(y, c0, c1_):
        yg = (y.astype(jnp.float32) * sg_ref[:, c0:c1_]).astype(jnp.bfloat16)
        return jnp.dot(yg, wo_ref[c0:c1_, :], preferred_element_type=jnp.float32)

    o = (x_ref[...] + seg(yf_ref[...], 0, c1) + seg(ym_ref[...], c1, c2)
         + seg(yd_ref[...], c2, D_MODEL))
    if final:
        o = _rms(o, fn_ref[...], NORM_EPS)
    o_ref[...] = o


def _out_proj(h, yf, ym, yd, sg, wo, final_norm, plan, final):
    rows = h.shape[0]
    tm = plan.tile
    row = lambda width: pl.BlockSpec((tm, width), lambda i: (i, 0))
    const = lambda shape: pl.BlockSpec(shape, lambda i: (0,) * len(shape),
                                       pipeline_mode=pl.Buffered(1))
    return pl.pallas_call(
        functools.partial(_out_proj_kernel, final=final),
        out_shape=jax.ShapeDtypeStruct((rows, D_MODEL), jnp.float32),
        grid=(rows // tm,),
        in_specs=[row(D_MODEL), row(F_WIDTH), row(MLA_WIDTH), row(DIFF_WIDTH), row(GATE_WIDTH),
                  const((D_MODEL, D_MODEL)), const((1, D_MODEL))],
        out_specs=row(D_MODEL),
        compiler_params=pltpu.CompilerParams(
            dimension_semantics=("arbitrary",), vmem_limit_bytes=_vmem_limit(48 << 20)),
        name="out_proj",
    )(h, yf, ym, yd, sg, wo, final_norm)


def _t5_bucket(rel):
    nb = REL_BUCKETS // 2
    max_exact = nb // 2
    ret = (rel > 0).astype(jnp.int32) * nb
    n = jnp.abs(rel)
    nf = jnp.maximum(n, 1).astype(jnp.float32)
    large = max_exact + (jnp.log(nf / max_exact) / math.log(REL_MAX_DIST / max_exact)
                         * (nb - max_exact)).astype(jnp.int32)
    large = jnp.minimum(large, nb - 1)
    return ret + jnp.where(n < max_exact, n, large)


def _bucket_tiles(tq):
    assert LANES >= REL_MAX_DIST
    o = jnp.arange(-BIAS_SIDE, tq // LANES + BIAS_SIDE, dtype=jnp.int32)[:, None, None] * LANES
    row = jnp.arange(tq, dtype=jnp.int32)[None, :, None]
    col = jnp.arange(LANES, dtype=jnp.int32)[None, None, :]
    return _t5_bucket(o + col - row)


def _rope_table(s_pad):
    pos = jnp.arange(s_pad, dtype=jnp.float32)
    inv_freq = ROPE_THETA ** (-jnp.arange(0, QK_ROPE, 2, dtype=jnp.float32) / QK_ROPE)
    ang = pos[:, None] * inv_freq[None, :]
    cos, sin = jnp.cos(ang), jnp.sin(ang)
    return jnp.concatenate([cos, cos, -sin, sin], axis=-1)


def _dft_angle_tables(n, rows, cols):
    r = (jnp.arange(rows, dtype=jnp.int32)[:, None] * jnp.arange(cols, dtype=jnp.int32)[None, :]) % n
    ang = r.astype(jnp.float32) * (2.0 * math.pi / n)
    return jnp.cos(ang), jnp.sin(ang)


def _seq_dft_matrix(plan):
    seq_len, s_pad, t = plan.seq_len, plan.s_pad, plan.tile
    na = s_pad // t
    k = jnp.arange(s_pad, dtype=jnp.int32)
    ra = (jnp.arange(na, dtype=jnp.int32)[:, None] * t * k[None, :]) % seq_len
    rb = (jnp.arange(t, dtype=jnp.int32)[:, None] * k[None, :]) % seq_len
    w = 2.0 * math.pi / seq_len
    aa = ra.astype(jnp.float32) * w
    bb = rb.astype(jnp.float32) * w
    ca, sa = jnp.cos(aa)[:, None, :], jnp.sin(aa)[:, None, :]
    cb, sb = jnp.cos(bb)[None], jnp.sin(bb)[None]
    scale = 1.0 / math.sqrt(seq_len)
    cosm = ((ca * cb - sa * sb) * scale).reshape(s_pad, s_pad)
    sinm = ((sa * cb + ca * sb) * scale).reshape(s_pad, s_pad)
    valid = (k[:, None] < seq_len) & (k[None, :] < seq_len)
    zero = jnp.zeros((), jnp.float32)
    cosm = jnp.where(valid, cosm, zero).astype(jnp.bfloat16)
    sinm = jnp.where(valid, sinm, zero).astype(jnp.bfloat16)
    chunk = plan.chunk
    both = jnp.stack([cosm, sinm], axis=0).reshape(2, s_pad, s_pad // chunk, chunk)
    return both.transpose(0, 2, 1, 3).reshape(2 * s_pad // chunk, s_pad, chunk)


def _layer_weights(l, w_in, w_uq, w_ukv, w_o):
    bf = jnp.bfloat16
    w = w_in[l]
    s = np.cumsum([0, F_WIDTH, Q_LORA, KV_LORA, QK_ROPE, DIFF_WIDTH, DIFF_WIDTH, DIFF_WIDTH,
                   GATE_WIDTH])
    uf, cq, ckv, kr, qd, kd, vd, gate = (w[:, s[i]:s[i + 1]] for i in range(8))
    half = QK_ROPE // 2
    kr_sw = jnp.concatenate([kr[:, half:], kr[:, :half]], axis=1)
    w_in_r = jnp.concatenate([uf, cq, ckv, kr, kr_sw, qd, kd, vd, gate], axis=1).astype(bf)
    wq = w_uq[l].reshape(Q_LORA, MLA_HEADS, QK_NOPE + QK_ROPE)
    rope = wq[..., QK_NOPE:]
    rope_sw = jnp.concatenate([rope[..., half:], rope[..., :half]], axis=-1)
    wq = jnp.concatenate([wq, rope_sw], axis=-1).transpose(1, 0, 2).astype(bf)
    wkv = w_ukv[l].reshape(KV_LORA, MLA_HEADS, QK_NOPE + V_HEAD).transpose(1, 0, 2).astype(bf)
    return w_in_r, wq, wkv, w_o[l].astype(bf)


def _encode(x, plan, meta_tokens, bias_tiles, final_norm, norm_w, q_norm, kv_norm, diff_norm,
            lam_vecs, pq, layer_w):
    batch, seq_len, s_pad = plan.batch, plan.seq_len, plan.s_pad
    rows = batch * s_pad
    meta = jnp.broadcast_to(meta_tokens[None], (batch, N_META, D_MODEL))
    pad = jnp.zeros((batch, s_pad - seq_len, D_MODEL), x.dtype)
    h = jnp.concatenate([meta, x, pad], axis=1).reshape(rows, D_MODEL)

    t1 = _rope_table(s_pad)
    cs = _seq_dft_matrix(plan)
    for l in range(DEPTH):
        w_in_r, wq, wkv, wo = layer_w[l]
        lam_init = 0.8 - 0.6 * math.exp(-0.3 * l)
        ab, cq, ckv, kr, qd, kd, vd, sg = _in_proj(
            h, norm_w[l][None], w_in_r, pq[l], q_norm[l][None], kv_norm[l][None], t1, plan)
        q, k, v = _mla_up(cq, ckv, kr, t1, wq, wkv, plan)
        yf = _seq_dft(cs, ab, plan)
        ym = _mla_attn(q, k, v, plan)
        yd = _diff_attn(qd, kd, vd, bias_tiles, lam_vecs[l], diff_norm[l][None], plan, lam_init)
        h = _out_proj(h, yf, ym, yd, sg, wo, final_norm[None], plan, final=(l == DEPTH - 1))
    return h.reshape(batch, s_pad, D_MODEL)[:, N_META:seq_len]


def kernel(x_prompt, x_sample, meta_tokens, rel_bias, final_norm, norm_w, w_in, w_fmix, q_norm,
           w_uq, kv_norm, w_ukv, lam_q1, lam_k1, lam_q2, lam_k2, diff_norm, w_o):
    cc, sc = _dft_angle_tables(F_GROUP_DIM, F_GROUP_DIM, F_GROUP_DIM)
    chan_cs = jnp.stack([cc, -sc]) * (1.0 / math.sqrt(F_GROUP_DIM))
    pq = _fold_fmix(chan_cs, w_fmix).astype(jnp.bfloat16)
    lam_vecs = jnp.stack([lam_q1, lam_k1, lam_q2, lam_k2], axis=1)
    layer_w = [_layer_weights(l, w_in, w_uq, w_ukv, w_o) for l in range(DEPTH)]
    outs = []
    bias_by_tile = {}
    for x in (x_prompt, x_sample):
        plan = _plan(x.shape[0], x.shape[1])
        if plan.tile not in bias_by_tile:
            bias_by_tile[plan.tile] = _bias_tiles(rel_bias, _bucket_tiles(plan.tile))
        outs.append(_encode(
            x, plan, meta_tokens, bias_by_tile[plan.tile], final_norm, norm_w, q_norm, kv_norm,
            diff_norm, lam_vecs, pq, layer_w))
    return tuple(outs)
```

```python
import functools
import math
from typing import NamedTuple

import jax
import jax.numpy as jnp
import numpy as np
from jax import lax
from jax.experimental import pallas as pl
from jax.experimental.pallas import tpu as pltpu

D_MODEL = 2048
DEPTH = 2
N_META = 16
F_GROUPS = 4
F_GROUP_DIM = 128
F_WIDTH = F_GROUPS * F_GROUP_DIM
MLA_HEADS = 8
Q_LORA = 768
KV_LORA = 512
QK_NOPE = 128
QK_ROPE = 64
V_HEAD = 128
MLA_WIDTH = MLA_HEADS * V_HEAD
ROPE_THETA = 10000.0
DIFF_HEADS = 4
DIFF_QK = 64
DIFF_V = 2 * DIFF_QK
DIFF_WIDTH = DIFF_HEADS * DIFF_V
REL_BUCKETS = 32
REL_MAX_DIST = 128
NORM_EPS = 1e-6
DIFF_NORM_EPS = 1e-5
GATE_WIDTH = D_MODEL

LOG2E = math.log2(math.e)
MLA_QSCALE = LOG2E / math.sqrt(QK_NOPE + QK_ROPE)
DIFF_QSCALE = LOG2E / math.sqrt(DIFF_QK)
MASK_VALUE = -1e30

LANES = 128
VMEM_BYTES_V7X = 64 * 1024 * 1024

C_UF = 0
C_CQ = C_UF + F_WIDTH
C_CKV = C_CQ + Q_LORA
C_KR = C_CKV + KV_LORA
C_QD = C_KR + 2 * QK_ROPE
C_KD = C_QD + DIFF_WIDTH
C_VD = C_KD + DIFF_WIDTH
C_GATE = C_VD + DIFF_WIDTH
IN_COLS = C_GATE + GATE_WIDTH

MAX_ROW_TILE = 384
MAX_KEY_CHUNK = 1408
HEADS_PER_STEP = 1
DFT_RADIX = 16
DFT_COL_TILE = 1024
DFT_ROW_TILE = 88
BIAS_SIDE = 2


class SeqPlan(NamedTuple):
    batch: int
    seq_len: int
    s_pad: int
    tile: int
    chunk: int


def _round_up(n, m):
    return -(-n // m) * m


def _largest_tile(s_pad, limit):
    n = s_pad // LANES
    return LANES * max(d for d in range(1, n + 1) if n % d == 0 and d * LANES <= limit)


def _plan(batch, s_real):
    seq_len = s_real + N_META
    tile = min(MAX_ROW_TILE, _round_up(seq_len, LANES))
    s_pad = _round_up(seq_len, tile)
    return SeqPlan(batch, seq_len, s_pad, tile, _largest_tile(s_pad, MAX_KEY_CHUNK))


def _vmem_limit(nbytes):
    return int(min(nbytes, VMEM_BYTES_V7X - (8 << 20)))


def _rms(x, g, eps):
    ms = jnp.mean(x * x, axis=-1, keepdims=True)
    return x * lax.rsqrt(ms + eps) * g


def _rope_fold(v, t1):
    w = v * t1
    return w + pltpu.roll(w, QK_ROPE, 1)


def _fold_fmix_kernel(cs_ref, w_ref, o_ref):
    w = w_ref[0, 0]
    o_ref[0, 0, :, :F_GROUP_DIM] = jnp.dot(
        cs_ref[0], w, preferred_element_type=jnp.float32,
        precision=lax.Precision.HIGHEST)
    o_ref[0, 0, :, F_GROUP_DIM:] = jnp.dot(
        cs_ref[1], w, preferred_element_type=jnp.float32,
        precision=lax.Precision.HIGHEST)


def _fold_fmix(chan_cs, w_fmix):
    c = F_GROUP_DIM
    return pl.pallas_call(
        _fold_fmix_kernel,
        out_shape=jax.ShapeDtypeStruct((DEPTH, F_GROUPS, c, 2 * c), jnp.float32),
        grid=(DEPTH, F_GROUPS),
        in_specs=[pl.BlockSpec((2, c, c), lambda l, g: (0, 0, 0)),
                  pl.BlockSpec((1, 1, c, c), lambda l, g: (l, g, 0, 0))],
        out_specs=pl.BlockSpec((1, 1, c, 2 * c), lambda l, g: (l, g, 0, 0)),
        name="fold_fmix",
    )(chan_cs, w_fmix)


def _bias_tiles_kernel(tab_ref, bucket_ref, o_ref):
    h = pl.program_id(1)
    bucket = bucket_ref[0]
    acc = jnp.zeros(bucket.shape, jnp.float32)
    for b in range(REL_BUCKETS):
        acc = jnp.where(bucket == b, tab_ref[b * DIFF_HEADS + h], acc)
    o_ref[0, 0] = acc * LOG2E


def _bias_tiles(rel_bias, bucket_tiles):
    nd, t, w = bucket_tiles.shape
    return pl.pallas_call(
        _bias_tiles_kernel,
        out_shape=jax.ShapeDtypeStruct((DIFF_HEADS, nd, t, w), jnp.float32),
        grid=(nd, DIFF_HEADS),
        in_specs=[pl.BlockSpec(memory_space=pltpu.SMEM),
                  pl.BlockSpec((1, t, w), lambda d, h: (d, 0, 0))],
        out_specs=pl.BlockSpec((1, 1, t, w), lambda d, h: (h, d, 0, 0)),
        name="bias_tiles",
    )(rel_bias.reshape(-1), bucket_tiles)


def _in_proj_kernel(x_ref, nw_ref, w_ref, pq_ref, qn_ref, kvn_ref, t1_ref,
                    ab_ref, cq_ref, ckv_ref, kr_ref, qd_ref, kd_ref, vd_ref, sg_ref):
    x = x_ref[...]
    xn = _rms(x, nw_ref[...], NORM_EPS).astype(jnp.bfloat16)

    def proj(c0, width):
        return jnp.dot(xn, w_ref[:, c0:c0 + width], preferred_element_type=jnp.float32)

    uf = proj(C_UF, F_WIDTH).astype(jnp.bfloat16)
    c = F_GROUP_DIM
    for g in range(F_GROUPS):
        ab = jnp.dot(uf[:, g * c:(g + 1) * c], pq_ref[g], preferred_element_type=jnp.float32)
        ab_ref[0, 0, :, g * c:(g + 1) * c] = ab[:, :c].astype(jnp.bfloat16)
        ab_ref[0, 1, :, g * c:(g + 1) * c] = ab[:, c:].astype(jnp.bfloat16)

    cq_ref[...] = _rms(proj(C_CQ, Q_LORA), qn_ref[...], NORM_EPS).astype(jnp.bfloat16)
    ckv_ref[...] = _rms(proj(C_CKV, KV_LORA), kvn_ref[...], NORM_EPS).astype(jnp.bfloat16)
    kr_ref[...] = _rope_fold(proj(C_KR, 2 * QK_ROPE), t1_ref[...]).astype(jnp.bfloat16)
    qd_ref[...] = (proj(C_QD, DIFF_WIDTH) * DIFF_QSCALE).astype(jnp.bfloat16)
    kd_ref[...] = proj(C_KD, DIFF_WIDTH).astype(jnp.bfloat16)
    vd_ref[...] = proj(C_VD, DIFF_WIDTH).astype(jnp.bfloat16)
    step = 512
    for c0 in range(0, GATE_WIDTH, step):
        gate = proj(C_GATE + c0, step)
        sg_ref[:, c0:c0 + step] = gate * jax.nn.sigmoid(gate)


def _in_proj(h, norm_w, w_in, pq, q_norm, kv_norm, t1, plan):
    rows = h.shape[0]
    tm = plan.tile
    nt = plan.s_pad // tm
    row = lambda width: pl.BlockSpec((tm, width), lambda i: (i, 0))
    const = lambda shape: pl.BlockSpec(shape, lambda i: (0,) * len(shape),
                                       pipeline_mode=pl.Buffered(1))
    bf = jnp.bfloat16
    out_shape = (
        jax.ShapeDtypeStruct((plan.batch, 2, plan.s_pad, F_WIDTH), bf),
        jax.ShapeDtypeStruct((rows, Q_LORA), bf),
        jax.ShapeDtypeStruct((rows, KV_LORA), bf),
        jax.ShapeDtypeStruct((rows, 2 * QK_ROPE), bf),
        jax.ShapeDtypeStruct((rows, DIFF_WIDTH), bf),
        jax.ShapeDtypeStruct((rows, DIFF_WIDTH), bf),
        jax.ShapeDtypeStruct((rows, DIFF_WIDTH), bf),
        jax.ShapeDtypeStruct((rows, GATE_WIDTH), jnp.float32),
    )
    out_specs = (
        pl.BlockSpec((1, 2, tm, F_WIDTH), lambda i: (i // nt, 0, i % nt, 0)),
        row(Q_LORA), row(KV_LORA), row(2 * QK_ROPE),
        row(DIFF_WIDTH), row(DIFF_WIDTH), row(DIFF_WIDTH), row(GATE_WIDTH),
    )
    in_specs = [
        row(D_MODEL),
        const((1, D_MODEL)),
        const((D_MODEL, IN_COLS)),
        const((F_GROUPS, F_GROUP_DIM, 2 * F_GROUP_DIM)),
        const((1, Q_LORA)),
        const((1, KV_LORA)),
        pl.BlockSpec((tm, 2 * QK_ROPE), lambda i: (i % nt, 0)),
    ]
    return pl.pallas_call(
        _in_proj_kernel,
        out_shape=out_shape,
        grid=(rows // tm,),
        in_specs=in_specs,
        out_specs=out_specs,
        compiler_params=pltpu.CompilerParams(
            dimension_semantics=("arbitrary",), vmem_limit_bytes=_vmem_limit(56 << 20)),
        name="in_proj",
    )(h, norm_w, w_in, pq, q_norm, kv_norm, t1)


def _mla_up_kernel(cq_ref, ckv_ref, kr_ref, t1_ref, wq_ref, wkv_ref, q_ref, k_ref, v_ref):
    cq = cq_ref[...]
    ckv = ckv_ref[...]
    t1 = t1_ref[...]
    k_rope = kr_ref[:, :QK_ROPE]
    for h in range(MLA_HEADS):
        r = jnp.dot(cq, wq_ref[h], preferred_element_type=jnp.float32) * MLA_QSCALE
        q_ref[h, :, :QK_NOPE] = r[:, :QK_NOPE].astype(jnp.bfloat16)
        rope = _rope_fold(r[:, QK_NOPE:], t1)
        q_ref[h, :, QK_NOPE:] = rope[:, :QK_ROPE].astype(jnp.bfloat16)
        kv = jnp.dot(ckv, wkv_ref[h], preferred_element_type=jnp.float32)
        k_ref[h, :, :QK_NOPE] = kv[:, :QK_NOPE].astype(jnp.bfloat16)
        k_ref[h, :, QK_NOPE:] = k_rope
        v_ref[h] = kv[:, QK_NOPE:].astype(jnp.bfloat16)


def _mla_up(cq, ckv, kr, t1, wq, wkv, plan):
    rows = cq.shape[0]
    tm = plan.tile
    nt = plan.s_pad // tm
    dqk = QK_NOPE + QK_ROPE
    bf = jnp.bfloat16
    row = lambda width: pl.BlockSpec((tm, width), lambda i: (i, 0))
    heads = lambda width: pl.BlockSpec((MLA_HEADS, tm, width), lambda i: (0, i, 0))
    const = lambda shape: pl.BlockSpec(shape, lambda i: (0,) * len(shape))
    return pl.pallas_call(
        _mla_up_kernel,
        out_shape=(jax.ShapeDtypeStruct((MLA_HEADS, rows, dqk), bf),
                   jax.ShapeDtypeStruct((MLA_HEADS, rows, dqk), bf),
                   jax.ShapeDtypeStruct((MLA_HEADS, rows, V_HEAD), bf)),
        grid=(rows // tm,),
        in_specs=[row(Q_LORA), row(KV_LORA), row(2 * QK_ROPE),
                  pl.BlockSpec((tm, 2 * QK_ROPE), lambda i: (i % nt, 0)),
                  const((MLA_HEADS, Q_LORA, QK_NOPE + 2 * QK_ROPE)),
                  const((MLA_HEADS, KV_LORA, QK_NOPE + V_HEAD))],
        out_specs=(heads(dqk), heads(dqk), heads(V_HEAD)),
        compiler_params=pltpu.CompilerParams(dimension_semantics=("arbitrary",)),
        name="mla_up",
    )(cq, ckv, kr, t1, wq, wkv)


def _dft_inner_kernel(g_ref, x_ref, o_ref):
    a = x_ref[0, 0]
    b = x_ref[0, 1]
    dot = functools.partial(jnp.dot, preferred_element_type=jnp.float32)
    o_ref[0, 0] = dot(g_ref[0, 0], a) + dot(g_ref[0, 1], b)
    o_ref[0, 1] = dot(g_ref[1, 0], a) + dot(g_ref[1, 1], b)


def _dft_inner(g, x):
    batch, _, m2, width = x.shape
    tn = DFT_COL_TILE
    return pl.pallas_call(
        _dft_inner_kernel,
        out_shape=jax.ShapeDtypeStruct((batch, 2, m2, width), jnp.float32),
        grid=(batch, width // tn),
        in_specs=[pl.BlockSpec((2, 2, m2, m2), lambda b, j: (0, 0, 0, 0)),
                  pl.BlockSpec((1, 2, m2, tn), lambda b, j: (b, 0, 0, j))],
        out_specs=pl.BlockSpec((1, 2, m2, tn), lambda b, j: (b, 0, 0, j)),
        compiler_params=pltpu.CompilerParams(
            dimension_semantics=("arbitrary", "arbitrary"),
            vmem_limit_bytes=_vmem_limit(40 << 20)),
        name="dft_inner",
    )(g, x)


def _dft_outer_kernel(tw_ref, y_ref, o_ref):
    def body(k1, carry):
        tw = tw_ref[k1]
        acc = jnp.zeros(o_ref.shape[2:], jnp.float32)
        for n1 in range(DFT_RADIX):
            cols = slice(n1 * F_WIDTH, (n1 + 1) * F_WIDTH)
            acc = acc + tw[:, n1:n1 + 1] * y_ref[0, 0, :, cols]
            acc = acc + tw[:, DFT_RADIX + n1:DFT_RADIX + n1 + 1] * y_ref[0, 1, :, cols]
        o_ref[0, k1] = acc
        return carry

    lax.fori_loop(0, DFT_RADIX, body, 0)


def _dft_outer(tw, y):
    batch, _, m2, width = y.shape
    t2 = max(d for d in range(8, m2 + 1, 8) if m2 % d == 0 and d <= DFT_ROW_TILE)
    return pl.pallas_call(
        _dft_outer_kernel,
        out_shape=jax.ShapeDtypeStruct((batch, DFT_RADIX, m2, F_WIDTH), jnp.float32),
        grid=(batch, m2 // t2),
        in_specs=[pl.BlockSpec((DFT_RADIX, t2, 2 * DFT_RADIX), lambda b, j: (0, j, 0)),
                  pl.BlockSpec((1, 2, t2, width), lambda b, j: (b, 0, j, 0))],
        out_specs=pl.BlockSpec((1, DFT_RADIX, t2, F_WIDTH), lambda b, j: (b, 0, j, 0)),
        compiler_params=pltpu.CompilerParams(
            dimension_semantics=("arbitrary", "arbitrary"),
            vmem_limit_bytes=_vmem_limit(40 << 20)),
        name="dft_outer",
    )(tw, y)


def _seq_dft(tables, ab, plan):
    g, tw = tables
    batch, s_pad, n2 = plan.batch, plan.s_pad, plan.seq_len // DFT_RADIX
    m2 = s_pad // DFT_RADIX
    y = _dft_inner(g, ab.reshape(batch, 2, m2, DFT_RADIX * F_WIDTH))
    out = _dft_outer(tw, y)[:, :, :n2].reshape(batch, DFT_RADIX * n2, F_WIDTH)
    out = jnp.pad(out, ((0, 0), (0, s_pad - DFT_RADIX * n2), (0, 0)))
    return out.reshape(batch * s_pad, F_WIDTH)


def _key_chunks(plan):
    k_end = _round_up(plan.seq_len, LANES)
    return [(c0, min(plan.chunk, k_end - c0)) for c0 in range(0, k_end, plan.chunk)]


def _mask_pad_keys(s, c0, seq_len):
    width = s.shape[-1]
    n_valid = seq_len - c0
    if n_valid >= width:
        return s
    a = width - LANES
    assert n_valid > a
    lane = lax.broadcasted_iota(jnp.int32, (s.shape[0], LANES), 1)
    tail = jnp.where(lane < n_valid - a, s[:, a:], MASK_VALUE)
    return jnp.concatenate([s[:, :a], tail], axis=1) if a else tail


def _softmax_step(s, v, state):
    if state is None:
        m_new = jnp.max(s, axis=-1, keepdims=True)
        p = jnp.exp2(s - m_new)
        l = jnp.sum(p, axis=-1, keepdims=True)
        acc = jnp.dot(p.astype(jnp.bfloat16), v, preferred_element_type=jnp.float32)
        return m_new, l, acc
    m, l, acc = state
    m_new = jnp.maximum(m, jnp.max(s, axis=-1, keepdims=True))
    p = jnp.exp2(s - m_new)
    alpha = jnp.exp2(m - m_new)
    l = alpha * l + jnp.sum(p, axis=-1, keepdims=True)
    acc = alpha * acc + jnp.dot(p.astype(jnp.bfloat16), v, preferred_element_type=jnp.float32)
    return m_new, l, acc


def _qk(q, k):
    return lax.dot_general(q, k, (((1,), (1,)), ((), ())), preferred_element_type=jnp.float32)


def _mla_attn_kernel(q_ref, k_ref, v_ref, o_ref, *, chunks, seq_len):
    heads = range(q_ref.shape[0])
    q = [q_ref[h] for h in heads]
    state = [None for _ in heads]
    for c0, width in chunks:
        for h in heads:
            s = _qk(q[h], k_ref[h, c0:c0 + width, :])
            s = _mask_pad_keys(s, c0, seq_len)
            state[h] = _softmax_step(s, v_ref[h, c0:c0 + width, :], state[h])
    for h in heads:
        _, l, acc = state[h]
        o_ref[:, h * V_HEAD:(h + 1) * V_HEAD] = (acc / l).astype(o_ref.dtype)


def _mla_attn(q, k, v, plan):
    tq = plan.tile
    nq = plan.s_pad // tq
    dqk = QK_NOPE + QK_ROPE
    hs = HEADS_PER_STEP
    return pl.pallas_call(
        functools.partial(_mla_attn_kernel, chunks=_key_chunks(plan), seq_len=plan.seq_len),
        out_shape=jax.ShapeDtypeStruct((plan.batch * plan.s_pad, MLA_WIDTH), jnp.bfloat16),
        grid=(plan.batch, MLA_HEADS // hs, nq),
        in_specs=[pl.BlockSpec((hs, tq, dqk), lambda b, h, i: (h, b * nq + i, 0)),
                  pl.BlockSpec((hs, plan.s_pad, dqk), lambda b, h, i: (h, b, 0),
                               pipeline_mode=pl.Buffered(1)),
                  pl.BlockSpec((hs, plan.s_pad, V_HEAD), lambda b, h, i: (h, b, 0),
                               pipeline_mode=pl.Buffered(1))],
        out_specs=pl.BlockSpec((tq, hs * V_HEAD), lambda b, h, i: (b * nq + i, h)),
        compiler_params=pltpu.CompilerParams(
            dimension_semantics=("arbitrary", "arbitrary", "arbitrary"),
            vmem_limit_bytes=_vmem_limit(56 << 20)),
        name="mla_attn",
    )(q, k, v)


def _diff_attn_kernel(q_ref, k_ref, v_ref, bt_ref, lam_ref, g_ref, o_ref,
                      *, chunks, seq_len, lam_init):
    tq = q_ref.shape[0]
    q_tiles = tq // LANES
    heads = range(q_ref.shape[1] // DIFF_V)
    i = pl.program_id(2)
    lane = lax.broadcasted_iota(jnp.int32, (tq, DIFF_V), 1)
    q2x = []
    for h in heads:
        qq = q_ref[:, h * DIFF_V:(h + 1) * DIFF_V].astype(jnp.float32)
        zero = jnp.zeros_like(qq)
        q2x.append(jnp.concatenate([jnp.where(lane < DIFF_QK, qq, zero),
                                    jnp.where(lane >= DIFF_QK, qq, zero)],
                                   axis=0).astype(jnp.bfloat16))
    state = [None for _ in heads]
    for c0, width in chunks:
        for h in heads:
            cols = slice(h * DIFF_V, (h + 1) * DIFF_V)
            s = _qk(q2x[h], k_ref[c0:c0 + width, cols])
            parts = []
            for g in range(width // LANES):
                o = (c0 // LANES + g) - i * q_tiles
                bias = bt_ref[h, jnp.clip(o, -BIAS_SIDE, q_tiles + BIAS_SIDE - 1) + BIAS_SIDE]
                sub = s[:, g * LANES:(g + 1) * LANES]
                parts.append(jnp.concatenate([sub[:tq] + bias, sub[tq:] + bias], axis=0))
            s = jnp.concatenate(parts, axis=1)
            s = _mask_pad_keys(s, c0, seq_len)
            state[h] = _softmax_step(s, v_ref[c0:c0 + width, cols], state[h])
    lv = lam_ref[...]
    lam = (jnp.exp(jnp.sum(lv[0:1] * lv[1:2], axis=-1, keepdims=True))
           - jnp.exp(jnp.sum(lv[2:3] * lv[3:4], axis=-1, keepdims=True)) + lam_init)
    for h in heads:
        _, l, acc = state[h]
        o = acc / l
        od = o[:tq] - lam * o[tq:]
        o_ref[:, h * DIFF_V:(h + 1) * DIFF_V] = (
            _rms(od, g_ref[...], DIFF_NORM_EPS) * (1.0 - lam_init)).astype(o_ref.dtype)


def _diff_attn(qd, kd, vd, bias_tiles, lam_vec, diff_norm, plan, lam_init):
    tq = plan.tile
    nq = plan.s_pad // tq
    nd = bias_tiles.shape[1]
    hs = HEADS_PER_STEP
    return pl.pallas_call(
        functools.partial(_diff_attn_kernel, chunks=_key_chunks(plan), seq_len=plan.seq_len,
                          lam_init=lam_init),
        out_shape=jax.ShapeDtypeStruct((plan.batch * plan.s_pad, DIFF_WIDTH), jnp.bfloat16),
        grid=(plan.batch, DIFF_HEADS // hs, nq),
        in_specs=[pl.BlockSpec((tq, hs * DIFF_V), lambda b, h, i: (b * nq + i, h)),
                  pl.BlockSpec((plan.s_pad, hs * DIFF_V), lambda b, h, i: (b, h),
                               pipeline_mode=pl.Buffered(1)),
                  pl.BlockSpec((plan.s_pad, hs * DIFF_V), lambda b, h, i: (b, h),
                               pipeline_mode=pl.Buffered(1)),
                  pl.BlockSpec((hs, nd, tq, LANES), lambda b, h, i: (h, 0, 0, 0),
                               pipeline_mode=pl.Buffered(1)),
                  pl.BlockSpec((4, DIFF_QK), lambda b, h, i: (0, 0)),
                  pl.BlockSpec((1, DIFF_V), lambda b, h, i: (0, 0))],
        out_specs=pl.BlockSpec((tq, hs * DIFF_V), lambda b, h, i: (b * nq + i, h)),
        compiler_params=pltpu.CompilerParams(
            dimension_semantics=("arbitrary", "arbitrary", "arbitrary"),
            vmem_limit_bytes=_vmem_limit(56 << 20)),
        name="diff_attn",
    )(qd, kd, vd, bias_tiles, lam_vec, diff_norm)


def _out_proj_kernel(x_ref, yf_ref, ym_ref, yd_ref, sg_ref, wo_ref, fn_ref, o_ref, *, final):
    c1 = F_WIDTH
    c2 = F_WIDTH + MLA_WIDTH

    def seg(y, c0, c1_):
        yg = (y.astype(jnp.float32) * sg_ref[:, c0:c1_]).astype(jnp.bfloat16)
        return jnp.dot(yg, wo_ref[c0:c1_, :], preferred_element_type=jnp.float32)

    o = (x_ref[...] + seg(yf_ref[...], 0, c1) + seg(ym_ref[...], c1, c2)
         + seg(yd_ref[...], c2, D_MODEL))
    if final:
        o = _rms(o, fn_ref[...], NORM_EPS)
    o_ref[...] = o


def _out_proj(h, yf, ym, yd, sg, wo, final_norm, plan, final):
    rows = h.shape[0]
    tm = plan.tile
    row = lambda width: pl.BlockSpec((tm, width), lambda i: (i, 0))
    const = lambda shape: pl.BlockSpec(shape, lambda i: (0,) * len(shape),
                                       pipeline_mode=pl.Buffered(1))
    return pl.pallas_call(
        functools.partial(_out_proj_kernel, final=final),
        out_shape=jax.ShapeDtypeStruct((rows, D_MODEL), jnp.float32),
        grid=(rows // tm,),
        in_specs=[row(D_MODEL), row(F_WIDTH), row(MLA_WIDTH), row(DIFF_WIDTH), row(GATE_WIDTH),
                  const((D_MODEL, D_MODEL)), const((1, D_MODEL))],
        out_specs=row(D_MODEL),
        compiler_params=pltpu.CompilerParams(
            dimension_semantics=("arbitrary",), vmem_limit_bytes=_vmem_limit(48 << 20)),
        name="out_proj",
    )(h, yf, ym, yd, sg, wo, final_norm)


def _t5_bucket(rel):
    nb = REL_BUCKETS // 2
    max_exact = nb // 2
    ret = (rel > 0).astype(jnp.int32) * nb
    n = jnp.abs(rel)
    nf = jnp.maximum(n, 1).astype(jnp.float32)
    large = max_exact + (jnp.log(nf / max_exact) / math.log(REL_MAX_DIST / max_exact)
                         * (nb - max_exact)).astype(jnp.int32)
    large = jnp.minimum(large, nb - 1)
    return ret + jnp.where(n < max_exact, n, large)


def _bucket_tiles(tq):
    assert LANES >= REL_MAX_DIST
    o = jnp.arange(-BIAS_SIDE, tq // LANES + BIAS_SIDE, dtype=jnp.int32)[:, None, None] * LANES
    row = jnp.arange(tq, dtype=jnp.int32)[None, :, None]
    col = jnp.arange(LANES, dtype=jnp.int32)[None, None, :]
    return _t5_bucket(o + col - row)


def _rope_table(s_pad):
    pos = jnp.arange(s_pad, dtype=jnp.float32)
    inv_freq = ROPE_THETA ** (-jnp.arange(0, QK_ROPE, 2, dtype=jnp.float32) / QK_ROPE)
    ang = pos[:, None] * inv_freq[None, :]
    cos, sin = jnp.cos(ang), jnp.sin(ang)
    return jnp.concatenate([cos, cos, -sin, sin], axis=-1)


def _dft_angle_tables(n, rows, cols):
    r = (jnp.arange(rows, dtype=jnp.int32)[:, None] * jnp.arange(cols, dtype=jnp.int32)[None, :]) % n
    ang = r.astype(jnp.float32) * (2.0 * math.pi / n)
    return jnp.cos(ang), jnp.sin(ang)


def _seq_dft_tables(plan):
    seq_len = plan.seq_len
    assert seq_len % DFT_RADIX == 0 and plan.s_pad % DFT_RADIX == 0
    n2 = seq_len // DFT_RADIX
    m2 = plan.s_pad // DFT_RADIX
    c, s = _dft_angle_tables(n2, m2, m2)
    idx = jnp.arange(m2, dtype=jnp.int32)
    valid = (idx[:, None] < n2) & (idx[None, :] < n2)
    zero = jnp.zeros((), jnp.float32)
    c = jnp.where(valid, c, zero)
    s = jnp.where(valid, s, zero)
    g = jnp.stack([jnp.stack([c, s]), jnp.stack([-s, c])]).astype(jnp.bfloat16)
    k = (jnp.arange(m2, dtype=jnp.int32)[None, :, None]
         + n2 * jnp.arange(DFT_RADIX, dtype=jnp.int32)[:, None, None])
    r = (k * jnp.arange(DFT_RADIX, dtype=jnp.int32)[None, None, :]) % seq_len
    ang = r.astype(jnp.float32) * (2.0 * math.pi / seq_len)
    tw = jnp.concatenate([jnp.cos(ang), jnp.sin(ang)], axis=-1) * (1.0 / math.sqrt(seq_len))
    return g, tw


def _layer_weights(l, w_in, w_uq, w_ukv, w_o):
    bf = jnp.bfloat16
    w = w_in[l]
    s = np.cumsum([0, F_WIDTH, Q_LORA, KV_LORA, QK_ROPE, DIFF_WIDTH, DIFF_WIDTH, DIFF_WIDTH,
                   GATE_WIDTH])
    uf, cq, ckv, kr, qd, kd, vd, gate = (w[:, s[i]:s[i + 1]] for i in range(8))
    half = QK_ROPE // 2
    kr_sw = jnp.concatenate([kr[:, half:], kr[:, :half]], axis=1)
    w_in_r = jnp.concatenate([uf, cq, ckv, kr, kr_sw, qd, kd, vd, gate], axis=1).astype(bf)
    wq = w_uq[l].reshape(Q_LORA, MLA_HEADS, QK_NOPE + QK_ROPE)
    rope = wq[..., QK_NOPE:]
    rope_sw = jnp.concatenate([rope[..., half:], rope[..., :half]], axis=-1)
    wq = jnp.concatenate([wq, rope_sw], axis=-1).transpose(1, 0, 2).astype(bf)
    wkv = w_ukv[l].reshape(KV_LORA, MLA_HEADS, QK_NOPE + V_HEAD).transpose(1, 0, 2).astype(bf)
    return w_in_r, wq, wkv, w_o[l].astype(bf)


def _encode(x, plan, meta_tokens, bias_tiles, final_norm, norm_w, q_norm, kv_norm, diff_norm,
            lam_vecs, pq, layer_w):
    batch, seq_len, s_pad = plan.batch, plan.seq_len, plan.s_pad
    rows = batch * s_pad
    meta = jnp.broadcast_to(meta_tokens[None], (batch, N_META, D_MODEL))
    pad = jnp.zeros((batch, s_pad - seq_len, D_MODEL), x.dtype)
    h = jnp.concatenate([meta, x, pad], axis=1).reshape(rows, D_MODEL)

    t1 = _rope_table(s_pad)
    dft_tables = _seq_dft_tables(plan)
    for l in range(DEPTH):
        w_in_r, wq, wkv, wo = layer_w[l]
        lam_init = 0.8 - 0.6 * math.exp(-0.3 * l)
        ab, cq, ckv, kr, qd, kd, vd, sg = _in_proj(
            h, norm_w[l][None], w_in_r, pq[l], q_norm[l][None], kv_norm[l][None], t1, plan)
        q, k, v = _mla_up(cq, ckv, kr, t1, wq, wkv, plan)
        yf = _seq_dft(dft_tables, ab, plan)
        ym = _mla_attn(q, k, v, plan)
        yd = _diff_attn(qd, kd, vd, bias_tiles, lam_vecs[l], diff_norm[l][None], plan, lam_init)
        h = _out_proj(h, yf, ym, yd, sg, wo, final_norm[None], plan, final=(l == DEPTH - 1))
    return h.reshape(batch, s_pad, D_MODEL)[:, N_META:seq_len]


def kernel(x_prompt, x_sample, meta_tokens, rel_bias, final_norm, norm_w, w_in, w_fmix, q_norm,
           w_uq, kv_norm, w_ukv, lam_q1, lam_k1, lam_q2, lam_k2, diff_norm, w_o):
    cc, sc = _dft_angle_tables(F_GROUP_DIM, F_GROUP_DIM, F_GROUP_DIM)
    chan_cs = jnp.stack([cc, -sc]) * (1.0 / math.sqrt(F_GROUP_DIM))
    pq = _fold_fmix(chan_cs, w_fmix).astype(jnp.bfloat16)
    lam_vecs = jnp.stack([lam_q1, lam_k1, lam_q2, lam_k2], axis=1)
    layer_w = [_layer_weights(l, w_in, w_uq, w_ukv, w_o) for l in range(DEPTH)]
    outs = []
    bias_by_tile = {}
    for x in (x_prompt, x_sample):
        plan = _plan(x.shape[0], x.shape[1])
        if plan.tile not in bias_by_tile:
            bias_by_tile[plan.tile] = _bias_tiles(rel_bias, _bucket_tiles(plan.tile))
        outs.append(_encode(
            x, plan, meta_tokens, bias_by_tile[plan.tile], final_norm, norm_w, q_norm, kv_norm,
            diff_norm, lam_vecs, pq, layer_w))
    return tuple(outs)
```

```python
import functools
import math
from typing import NamedTuple

import jax
import jax.numpy as jnp
import numpy as np
from jax import lax
from jax.experimental import pallas as pl
from jax.experimental.pallas import tpu as pltpu

D_MODEL = 2048
DEPTH = 2
N_META = 16
F_GROUPS = 4
F_GROUP_DIM = 128
F_WIDTH = F_GROUPS * F_GROUP_DIM
MLA_HEADS = 8
Q_LORA = 768
KV_LORA = 512
QK_NOPE = 128
QK_ROPE = 64
V_HEAD = 128
MLA_WIDTH = MLA_HEADS * V_HEAD
ROPE_THETA = 10000.0
DIFF_HEADS = 4
DIFF_QK = 64
DIFF_V = 2 * DIFF_QK
DIFF_WIDTH = DIFF_HEADS * DIFF_V
REL_BUCKETS = 32
REL_MAX_DIST = 128
NORM_EPS = 1e-6
DIFF_NORM_EPS = 1e-5
GATE_WIDTH = D_MODEL

LOG2E = math.log2(math.e)
MLA_QSCALE = LOG2E / math.sqrt(QK_NOPE + QK_ROPE)
DIFF_QSCALE = LOG2E / math.sqrt(DIFF_QK)
MASK_VALUE = -1e30

LANES = 128
VMEM_BYTES_V7X = 64 * 1024 * 1024

C_UF = 0
C_CQ = C_UF + F_WIDTH
C_CKV = C_CQ + Q_LORA
C_KR = C_CKV + KV_LORA
C_QD = C_KR + 2 * QK_ROPE
C_KD = C_QD + DIFF_WIDTH
C_VD = C_KD + DIFF_WIDTH
C_GATE = C_VD + DIFF_WIDTH
IN_COLS = C_GATE + GATE_WIDTH

MAX_ROW_TILE = 384
MAX_KEY_CHUNK = 1408
DFT_RADIX = 16
DFT_COL_TILE = 1024
DFT_ROW_TILE = 88
BIAS_SIDE = 2


class SeqPlan(NamedTuple):
    batch: int
    seq_len: int
    s_pad: int
    tile: int
    chunk: int


def _round_up(n, m):
    return -(-n // m) * m


def _largest_tile(s_pad, limit):
    n = s_pad // LANES
    return LANES * max(d for d in range(1, n + 1) if n % d == 0 and d * LANES <= limit)


def _plan(batch, s_real):
    seq_len = s_real + N_META
    tile = min(MAX_ROW_TILE, _round_up(seq_len, LANES))
    s_pad = _round_up(seq_len, tile)
    return SeqPlan(batch, seq_len, s_pad, tile, _largest_tile(s_pad, MAX_KEY_CHUNK))


def _vmem_limit(nbytes):
    return int(min(nbytes, VMEM_BYTES_V7X - (8 << 20)))


def _rms(x, g, eps):
    ms = jnp.mean(x * x, axis=-1, keepdims=True)
    return x * lax.rsqrt(ms + eps) * g


def _rope_fold(v, t1):
    w = v * t1
    return w + pltpu.roll(w, QK_ROPE, 1)


def _fold_fmix_kernel(cs_ref, w_ref, o_ref):
    w = w_ref[0, 0]
    o_ref[0, 0, :, :F_GROUP_DIM] = jnp.dot(
        cs_ref[0], w, preferred_element_type=jnp.float32,
        precision=lax.Precision.HIGHEST)
    o_ref[0, 0, :, F_GROUP_DIM:] = jnp.dot(
        cs_ref[1], w, preferred_element_type=jnp.float32,
        precision=lax.Precision.HIGHEST)


def _fold_fmix(chan_cs, w_fmix):
    c = F_GROUP_DIM
    return pl.pallas_call(
        _fold_fmix_kernel,
        out_shape=jax.ShapeDtypeStruct((DEPTH, F_GROUPS, c, 2 * c), jnp.float32),
        grid=(DEPTH, F_GROUPS),
        in_specs=[pl.BlockSpec((2, c, c), lambda l, g: (0, 0, 0)),
                  pl.BlockSpec((1, 1, c, c), lambda l, g: (l, g, 0, 0))],
        out_specs=pl.BlockSpec((1, 1, c, 2 * c), lambda l, g: (l, g, 0, 0)),
        name="fold_fmix",
    )(chan_cs, w_fmix)


def _bias_tiles_kernel(tab_ref, bucket_ref, o_ref):
    h = pl.program_id(1)
    bucket = bucket_ref[0]
    acc = jnp.zeros(bucket.shape, jnp.float32)
    for b in range(REL_BUCKETS):
        acc = jnp.where(bucket == b, tab_ref[b * DIFF_HEADS + h], acc)
    o_ref[0, 0] = acc * LOG2E


def _bias_tiles(rel_bias, bucket_tiles):
    nd, t, w = bucket_tiles.shape
    return pl.pallas_call(
        _bias_tiles_kernel,
        out_shape=jax.ShapeDtypeStruct((DIFF_HEADS, nd, t, w), jnp.float32),
        grid=(nd, DIFF_HEADS),
        in_specs=[pl.BlockSpec(memory_space=pltpu.SMEM),
                  pl.BlockSpec((1, t, w), lambda d, h: (d, 0, 0))],
        out_specs=pl.BlockSpec((1, 1, t, w), lambda d, h: (h, d, 0, 0)),
        name="bias_tiles",
    )(rel_bias.reshape(-1), bucket_tiles)


def _in_proj_kernel(x_ref, nw_ref, w_ref, pq_ref, qn_ref, kvn_ref, t1_ref,
                    ab_ref, cq_ref, ckv_ref, kr_ref, qd_ref, kd_ref, vd_ref, sg_ref):
    x = x_ref[...]
    xn = _rms(x, nw_ref[...], NORM_EPS).astype(jnp.bfloat16)

    def proj(c0, width):
        return jnp.dot(xn, w_ref[:, c0:c0 + width], preferred_element_type=jnp.float32)

    uf = proj(C_UF, F_WIDTH).astype(jnp.bfloat16)
    c = F_GROUP_DIM
    for g in range(F_GROUPS):
        ab = jnp.dot(uf[:, g * c:(g + 1) * c], pq_ref[g], preferred_element_type=jnp.float32)
        ab_ref[0, 0, :, g * c:(g + 1) * c] = ab[:, :c].astype(jnp.bfloat16)
        ab_ref[0, 1, :, g * c:(g + 1) * c] = ab[:, c:].astype(jnp.bfloat16)

    cq_ref[...] = _rms(proj(C_CQ, Q_LORA), qn_ref[...], NORM_EPS).astype(jnp.bfloat16)
    ckv_ref[...] = _rms(proj(C_CKV, KV_LORA), kvn_ref[...], NORM_EPS).astype(jnp.bfloat16)
    kr_ref[...] = _rope_fold(proj(C_KR, 2 * QK_ROPE), t1_ref[...]).astype(jnp.bfloat16)
    qd_ref[...] = (proj(C_QD, DIFF_WIDTH) * DIFF_QSCALE).astype(jnp.bfloat16)
    kd_ref[...] = proj(C_KD, DIFF_WIDTH).astype(jnp.bfloat16)
    vd_ref[...] = proj(C_VD, DIFF_WIDTH).astype(jnp.bfloat16)
    step = 512
    for c0 in range(0, GATE_WIDTH, step):
        gate = proj(C_GATE + c0, step)
        sg_ref[:, c0:c0 + step] = (gate * jax.nn.sigmoid(gate)).astype(sg_ref.dtype)


def _in_proj(h, norm_w, w_in, pq, q_norm, kv_norm, t1, plan):
    rows = h.shape[0]
    tm = plan.tile
    nt = plan.s_pad // tm
    row = lambda width: pl.BlockSpec((tm, width), lambda i: (i, 0))
    const = lambda shape: pl.BlockSpec(shape, lambda i: (0,) * len(shape),
                                       pipeline_mode=pl.Buffered(1))
    bf = jnp.bfloat16
    out_shape = (
        jax.ShapeDtypeStruct((plan.batch, 2, plan.s_pad, F_WIDTH), bf),
        jax.ShapeDtypeStruct((rows, Q_LORA), bf),
        jax.ShapeDtypeStruct((rows, KV_LORA), bf),
        jax.ShapeDtypeStruct((rows, 2 * QK_ROPE), bf),
        jax.ShapeDtypeStruct((rows, DIFF_WIDTH), bf),
        jax.ShapeDtypeStruct((rows, DIFF_WIDTH), bf),
        jax.ShapeDtypeStruct((rows, DIFF_WIDTH), bf),
        jax.ShapeDtypeStruct((rows, GATE_WIDTH), bf),
    )
    out_specs = (
        pl.BlockSpec((1, 2, tm, F_WIDTH), lambda i: (i // nt, 0, i % nt, 0)),
        row(Q_LORA), row(KV_LORA), row(2 * QK_ROPE),
        row(DIFF_WIDTH), row(DIFF_WIDTH), row(DIFF_WIDTH), row(GATE_WIDTH),
    )
    in_specs = [
        row(D_MODEL),
        const((1, D_MODEL)),
        const((D_MODEL, IN_COLS)),
        const((F_GROUPS, F_GROUP_DIM, 2 * F_GROUP_DIM)),
        const((1, Q_LORA)),
        const((1, KV_LORA)),
        pl.BlockSpec((tm, 2 * QK_ROPE), lambda i: (i % nt, 0)),
    ]
    return pl.pallas_call(
        _in_proj_kernel,
        out_shape=out_shape,
        grid=(rows // tm,),
        in_specs=in_specs,
        out_specs=out_specs,
        compiler_params=pltpu.CompilerParams(
            dimension_semantics=("arbitrary",), vmem_limit_bytes=_vmem_limit(56 << 20)),
        name="in_proj",
    )(h, norm_w, w_in, pq, q_norm, kv_norm, t1)


def _mla_up_kernel(cq_ref, ckv_ref, kr_ref, t1_ref, wq_ref, wkv_ref, q_ref, k_ref, v_ref):
    cq = cq_ref[...]
    ckv = ckv_ref[...]
    t1 = t1_ref[...]
    k_rope = kr_ref[:, :QK_ROPE]
    for h in range(MLA_HEADS):
        r = jnp.dot(cq, wq_ref[h], preferred_element_type=jnp.float32) * MLA_QSCALE
        q_ref[h, :, :QK_NOPE] = r[:, :QK_NOPE].astype(jnp.bfloat16)
        rope = _rope_fold(r[:, QK_NOPE:], t1)
        q_ref[h, :, QK_NOPE:] = rope[:, :QK_ROPE].astype(jnp.bfloat16)
        kv = jnp.dot(ckv, wkv_ref[h], preferred_element_type=jnp.float32)
        k_ref[h, :, :QK_NOPE] = kv[:, :QK_NOPE].astype(jnp.bfloat16)
        k_ref[h, :, QK_NOPE:] = k_rope
        v_ref[h] = kv[:, QK_NOPE:].astype(jnp.bfloat16)


def _mla_up(cq, ckv, kr, t1, wq, wkv, plan):
    rows = cq.shape[0]
    tm = plan.tile
    nt = plan.s_pad // tm
    dqk = QK_NOPE + QK_ROPE
    bf = jnp.bfloat16
    row = lambda width: pl.BlockSpec((tm, width), lambda i: (i, 0))
    heads = lambda width: pl.BlockSpec((MLA_HEADS, tm, width), lambda i: (0, i, 0))
    const = lambda shape: pl.BlockSpec(shape, lambda i: (0,) * len(shape))
    return pl.pallas_call(
        _mla_up_kernel,
        out_shape=(jax.ShapeDtypeStruct((MLA_HEADS, rows, dqk), bf),
                   jax.ShapeDtypeStruct((MLA_HEADS, rows, dqk), bf),
                   jax.ShapeDtypeStruct((MLA_HEADS, rows, V_HEAD), bf)),
        grid=(rows // tm,),
        in_specs=[row(Q_LORA), row(KV_LORA), row(2 * QK_ROPE),
                  pl.BlockSpec((tm, 2 * QK_ROPE), lambda i: (i % nt, 0)),
                  const((MLA_HEADS, Q_LORA, QK_NOPE + 2 * QK_ROPE)),
                  const((MLA_HEADS, KV_LORA, QK_NOPE + V_HEAD))],
        out_specs=(heads(dqk), heads(dqk), heads(V_HEAD)),
        compiler_params=pltpu.CompilerParams(dimension_semantics=("arbitrary",)),
        name="mla_up",
    )(cq, ckv, kr, t1, wq, wkv)


def _dft_inner_kernel(g_ref, x_ref, o_ref):
    a = x_ref[0, 0]
    b = x_ref[0, 1]
    dot = functools.partial(jnp.dot, preferred_element_type=jnp.float32)
    o_ref[0, 0] = dot(g_ref[0, 0], a) + dot(g_ref[0, 1], b)
    o_ref[0, 1] = dot(g_ref[1, 0], a) + dot(g_ref[1, 1], b)


def _dft_inner(g, x):
    batch, _, m2, width = x.shape
    tn = DFT_COL_TILE
    return pl.pallas_call(
        _dft_inner_kernel,
        out_shape=jax.ShapeDtypeStruct((batch, 2, m2, width), jnp.float32),
        grid=(batch, width // tn),
        in_specs=[pl.BlockSpec((2, 2, m2, m2), lambda b, j: (0, 0, 0, 0)),
                  pl.BlockSpec((1, 2, m2, tn), lambda b, j: (b, 0, 0, j))],
        out_specs=pl.BlockSpec((1, 2, m2, tn), lambda b, j: (b, 0, 0, j)),
        compiler_params=pltpu.CompilerParams(
            dimension_semantics=("arbitrary", "arbitrary"),
            vmem_limit_bytes=_vmem_limit(40 << 20)),
        name="dft_inner",
    )(g, x)


def _dft_outer_kernel(tw_ref, y_ref, o_ref):
    def body(k1, carry):
        tw = tw_ref[k1]
        acc = jnp.zeros(o_ref.shape[2:], jnp.float32)
        for n1 in range(DFT_RADIX):
            cols = slice(n1 * F_WIDTH, (n1 + 1) * F_WIDTH)
            acc = acc + tw[:, n1:n1 + 1] * y_ref[0, 0, :, cols]
            acc = acc + tw[:, DFT_RADIX + n1:DFT_RADIX + n1 + 1] * y_ref[0, 1, :, cols]
        o_ref[0, k1] = acc
        return carry

    lax.fori_loop(0, DFT_RADIX, body, 0)


def _dft_outer(tw, y):
    batch, _, m2, width = y.shape
    t2 = max(d for d in range(8, m2 + 1, 8) if m2 % d == 0 and d <= DFT_ROW_TILE)
    return pl.pallas_call(
        _dft_outer_kernel,
        out_shape=jax.ShapeDtypeStruct((batch, DFT_RADIX, m2, F_WIDTH), jnp.float32),
        grid=(batch, m2 // t2),
        in_specs=[pl.BlockSpec((DFT_RADIX, t2, 2 * DFT_RADIX), lambda b, j: (0, j, 0)),
                  pl.BlockSpec((1, 2, t2, width), lambda b, j: (b, 0, j, 0))],
        out_specs=pl.BlockSpec((1, DFT_RADIX, t2, F_WIDTH), lambda b, j: (b, 0, j, 0)),
        compiler_params=pltpu.CompilerParams(
            dimension_semantics=("arbitrary", "arbitrary"),
            vmem_limit_bytes=_vmem_limit(40 << 20)),
        name="dft_outer",
    )(tw, y)


def _seq_dft(tables, ab, plan):
    g, tw = tables
    batch, s_pad, n2 = plan.batch, plan.s_pad, plan.seq_len // DFT_RADIX
    m2 = s_pad // DFT_RADIX
    y = _dft_inner(g, ab.reshape(batch, 2, m2, DFT_RADIX * F_WIDTH))
    out = _dft_outer(tw, y)[:, :, :n2].reshape(batch, DFT_RADIX * n2, F_WIDTH)
    out = jnp.pad(out, ((0, 0), (0, s_pad - DFT_RADIX * n2), (0, 0)))
    return out.reshape(batch * s_pad, F_WIDTH)


def _key_chunks(plan):
    k_end = _round_up(plan.seq_len, LANES)
    return [(c0, min(plan.chunk, k_end - c0)) for c0 in range(0, k_end, plan.chunk)]


def _mask_pad_keys(s, c0, seq_len):
    width = s.shape[-1]
    n_valid = seq_len - c0
    if n_valid >= width:
        return s
    a = width - LANES
    assert n_valid > a
    lane = lax.broadcasted_iota(jnp.int32, (s.shape[0], LANES), 1)
    tail = jnp.where(lane < n_valid - a, s[:, a:], MASK_VALUE)
    return jnp.concatenate([s[:, :a], tail], axis=1) if a else tail


def _softmax_step(s, v, state):
    if state is None:
        m_new = jnp.max(s, axis=-1, keepdims=True)
        p = jnp.exp2(s - m_new)
        l = jnp.sum(p, axis=-1, keepdims=True)
        acc = jnp.dot(p.astype(jnp.bfloat16), v, preferred_element_type=jnp.float32)
        return m_new, l, acc
    m, l, acc = state
    m_new = jnp.maximum(m, jnp.max(s, axis=-1, keepdims=True))
    p = jnp.exp2(s - m_new)
    alpha = jnp.exp2(m - m_new)
    l = alpha * l + jnp.sum(p, axis=-1, keepdims=True)
    acc = alpha * acc + jnp.dot(p.astype(jnp.bfloat16), v, preferred_element_type=jnp.float32)
    return m_new, l, acc


def _qk(q, k):
    return lax.dot_general(q, k, (((1,), (1,)), ((), ())), preferred_element_type=jnp.float32)


def _tile_rows(tile, tq):
    return pl.ds(pl.multiple_of(tile * tq, tq), tq)


def _attn_pipeline(n_tiles, head, tail):
    def body(i, carry):
        tail(jnp.maximum(i - 1, 0))
        head(i)
        return carry

    lax.fori_loop(0, n_tiles, body, 0)
    tail(n_tiles - 1)


def _init_attn_state(s_last, m_sc, l_sc, acc_sc):
    s_last[...] = jnp.zeros(s_last.shape, jnp.float32)
    m_sc[...] = jnp.zeros(m_sc.shape, jnp.float32)
    l_sc[...] = jnp.ones(l_sc.shape, jnp.float32)
    acc_sc[...] = jnp.zeros(acc_sc.shape, jnp.float32)


def _store_attn_state(state, m_sc, l_sc, acc_sc):
    if state is None:
        m_sc[...] = jnp.full(m_sc.shape, MASK_VALUE, jnp.float32)
        l_sc[...] = jnp.zeros(l_sc.shape, jnp.float32)
        acc_sc[...] = jnp.zeros(acc_sc.shape, jnp.float32)
    else:
        m_sc[...], l_sc[...], acc_sc[...] = state


def _mla_attn_kernel(q_ref, k_ref, v_ref, o_ref, s_last, m_sc, l_sc, acc_sc,
                     *, chunks, seq_len, tq):
    c_last, w_last = chunks[-1]
    _init_attn_state(s_last, m_sc, l_sc, acc_sc)

    def head(tile):
        q = q_ref[0, _tile_rows(tile, tq), :]
        state = None
        for c0, width in chunks[:-1]:
            s = _qk(q, k_ref[0, c0:c0 + width, :])
            state = _softmax_step(s, v_ref[0, c0:c0 + width, :], state)
        s = _qk(q, k_ref[0, c_last:c_last + w_last, :])
        s_last[...] = _mask_pad_keys(s, c_last, seq_len)
        _store_attn_state(state, m_sc, l_sc, acc_sc)

    def tail(tile):
        _, l, acc = _softmax_step(s_last[...], v_ref[0, c_last:c_last + w_last, :],
                                  (m_sc[...], l_sc[...], acc_sc[...]))
        o_ref[_tile_rows(tile, tq), :] = (acc / l).astype(o_ref.dtype)

    _attn_pipeline(q_ref.shape[1] // tq, head, tail)


def _mla_attn(q, k, v, plan):
    tq = plan.tile
    dqk = QK_NOPE + QK_ROPE
    chunks = _key_chunks(plan)
    per_head = lambda width: pl.BlockSpec((1, plan.s_pad, width), lambda b, h: (h, b, 0))
    return pl.pallas_call(
        functools.partial(_mla_attn_kernel, chunks=chunks, seq_len=plan.seq_len, tq=tq),
        out_shape=jax.ShapeDtypeStruct((plan.batch * plan.s_pad, MLA_WIDTH), jnp.bfloat16),
        grid=(plan.batch, MLA_HEADS),
        in_specs=[per_head(dqk), per_head(dqk), per_head(V_HEAD)],
        out_specs=pl.BlockSpec((plan.s_pad, V_HEAD), lambda b, h: (b, h)),
        scratch_shapes=[pltpu.VMEM((tq, chunks[-1][1]), jnp.float32),
                        pltpu.VMEM((tq, 1), jnp.float32),
                        pltpu.VMEM((tq, 1), jnp.float32),
                        pltpu.VMEM((tq, V_HEAD), jnp.float32)],
        compiler_params=pltpu.CompilerParams(
            dimension_semantics=("arbitrary", "arbitrary"),
            vmem_limit_bytes=_vmem_limit(56 << 20)),
        name="mla_attn",
    )(q, k, v)


def _diff_attn_kernel(q_ref, k_ref, v_ref, bt_ref, lam_ref, g_ref, o_ref,
                      s_last, m_sc, l_sc, acc_sc, *, chunks, seq_len, tq, lam_init):
    q_tiles = tq // LANES
    c_last, w_last = chunks[-1]
    _init_attn_state(s_last, m_sc, l_sc, acc_sc)
    lv = lam_ref[...]
    lam = (jnp.exp(jnp.sum(lv[0:1] * lv[1:2], axis=-1, keepdims=True))
           - jnp.exp(jnp.sum(lv[2:3] * lv[3:4], axis=-1, keepdims=True)) + lam_init)

    def scores(q2x, tile, c0, width):
        s = _qk(q2x, k_ref[c0:c0 + width, :])
        parts = []
        for g in range(width // LANES):
            o = (c0 // LANES + g) - tile * q_tiles
            bias = bt_ref[0, jnp.clip(o, -BIAS_SIDE, q_tiles + BIAS_SIDE - 1) + BIAS_SIDE]
            sub = s[:, g * LANES:(g + 1) * LANES]
            parts.append(jnp.concatenate([sub[:tq] + bias, sub[tq:] + bias], axis=0))
        return jnp.concatenate(parts, axis=1)

    def head(tile):
        qq = q_ref[_tile_rows(tile, tq), :].astype(jnp.float32)
        lane = lax.broadcasted_iota(jnp.int32, qq.shape, 1)
        zero = jnp.zeros_like(qq)
        q2x = jnp.concatenate([jnp.where(lane < DIFF_QK, qq, zero),
                               jnp.where(lane >= DIFF_QK, qq, zero)], axis=0).astype(jnp.bfloat16)
        state = None
        for c0, width in chunks[:-1]:
            state = _softmax_step(scores(q2x, tile, c0, width), v_ref[c0:c0 + width, :], state)
        s_last[...] = _mask_pad_keys(scores(q2x, tile, c_last, w_last), c_last, seq_len)
        _store_attn_state(state, m_sc, l_sc, acc_sc)

    def tail(tile):
        _, l, acc = _softmax_step(s_last[...], v_ref[c_last:c_last + w_last, :],
                                  (m_sc[...], l_sc[...], acc_sc[...]))
        o = acc / l
        od = o[:tq] - lam * o[tq:]
        o_ref[_tile_rows(tile, tq), :] = (
            _rms(od, g_ref[...], DIFF_NORM_EPS) * (1.0 - lam_init)).astype(o_ref.dtype)

    _attn_pipeline(q_ref.shape[0] // tq, head, tail)


def _diff_attn(qd, kd, vd, bias_tiles, lam_vec, diff_norm, plan, lam_init):
    tq = plan.tile
    nd = bias_tiles.shape[1]
    chunks = _key_chunks(plan)
    per_head = pl.BlockSpec((plan.s_pad, DIFF_V), lambda b, h: (b, h))
    return pl.pallas_call(
        functools.partial(_diff_attn_kernel, chunks=chunks, seq_len=plan.seq_len, tq=tq,
                          lam_init=lam_init),
        out_shape=jax.ShapeDtypeStruct((plan.batch * plan.s_pad, DIFF_WIDTH), jnp.bfloat16),
        grid=(plan.batch, DIFF_HEADS),
        in_specs=[per_head, per_head, per_head,
                  pl.BlockSpec((1, nd, tq, LANES), lambda b, h: (h, 0, 0, 0)),
                  pl.BlockSpec((4, DIFF_QK), lambda b, h: (0, 0)),
                  pl.BlockSpec((1, DIFF_V), lambda b, h: (0, 0))],
        out_specs=per_head,
        scratch_shapes=[pltpu.VMEM((2 * tq, chunks[-1][1]), jnp.float32),
                        pltpu.VMEM((2 * tq, 1), jnp.float32),
                        pltpu.VMEM((2 * tq, 1), jnp.float32),
                        pltpu.VMEM((2 * tq, DIFF_V), jnp.float32)],
        compiler_params=pltpu.CompilerParams(
            dimension_semantics=("arbitrary", "arbitrary"),
            vmem_limit_bytes=_vmem_limit(56 << 20)),
        name="diff_attn",
    )(qd, kd, vd, bias_tiles, lam_vec, diff_norm)


def _out_proj_kernel(x_ref, yf_ref, ym_ref, yd_ref, sg_ref, wo_ref, fn_ref, o_ref, *, final):
    c1 = F_WIDTH
    c2 = F_WIDTH + MLA_WIDTH

    def seg(y, c0, c1_):
        yg = (y.astype(jnp.float32) * sg_ref[:, c0:c1_].astype(jnp.float32)).astype(jnp.bfloat16)
        return jnp.dot(yg, wo_ref[c0:c1_, :], preferred_element_type=jnp.float32)

    o = (x_ref[...] + seg(yf_ref[...], 0, c1) + seg(ym_ref[...], c1, c2)
         + seg(yd_ref[...], c2, D_MODEL))
    if final:
        o = _rms(o, fn_ref[...], NORM_EPS)
    o_ref[...] = o


def _out_proj(h, yf, ym, yd, sg, wo, final_norm, plan, final):
    rows = h.shape[0]
    tm = plan.tile
    row = lambda width: pl.BlockSpec((tm, width), lambda i: (i, 0))
    const = lambda shape: pl.BlockSpec(shape, lambda i: (0,) * len(shape),
                                       pipeline_mode=pl.Buffered(1))
    return pl.pallas_call(
        functools.partial(_out_proj_kernel, final=final),
        out_shape=jax.ShapeDtypeStruct((rows, D_MODEL), jnp.float32),
        grid=(rows // tm,),
        in_specs=[row(D_MODEL), row(F_WIDTH), row(MLA_WIDTH), row(DIFF_WIDTH), row(GATE_WIDTH),
                  const((D_MODEL, D_MODEL)), const((1, D_MODEL))],
        out_specs=row(D_MODEL),
        compiler_params=pltpu.CompilerParams(
            dimension_semantics=("arbitrary",), vmem_limit_bytes=_vmem_limit(48 << 20)),
        name="out_proj",
    )(h, yf, ym, yd, sg, wo, final_norm)


def _t5_bucket(rel):
    nb = REL_BUCKETS // 2
    max_exact = nb // 2
    ret = (rel > 0).astype(jnp.int32) * nb
    n = jnp.abs(rel)
    nf = jnp.maximum(n, 1).astype(jnp.float32)
    large = max_exact + (jnp.log(nf / max_exact) / math.log(REL_MAX_DIST / max_exact)
                         * (nb - max_exact)).astype(jnp.int32)
    large = jnp.minimum(large, nb - 1)
    return ret + jnp.where(n < max_exact, n, large)


def _bucket_tiles(tq):
    assert LANES >= REL_MAX_DIST
    o = jnp.arange(-BIAS_SIDE, tq // LANES + BIAS_SIDE, dtype=jnp.int32)[:, None, None] * LANES
    row = jnp.arange(tq, dtype=jnp.int32)[None, :, None]
    col = jnp.arange(LANES, dtype=jnp.int32)[None, None, :]
    return _t5_bucket(o + col - row)


def _rope_table(s_pad):
    pos = jnp.arange(s_pad, dtype=jnp.float32)
    inv_freq = ROPE_THETA ** (-jnp.arange(0, QK_ROPE, 2, dtype=jnp.float32) / QK_ROPE)
    ang = pos[:, None] * inv_freq[None, :]
    cos, sin = jnp.cos(ang), jnp.sin(ang)
    return jnp.concatenate([cos, cos, -sin, sin], axis=-1)


def _dft_angle_tables(n, rows, cols):
    r = (jnp.arange(rows, dtype=jnp.int32)[:, None] * jnp.arange(cols, dtype=jnp.int32)[None, :]) % n
    ang = r.astype(jnp.float32) * (2.0 * math.pi / n)
    return jnp.cos(ang), jnp.sin(ang)


def _seq_dft_tables(plan):
    seq_len = plan.seq_len
    assert seq_len % DFT_RADIX == 0 and plan.s_pad % DFT_RADIX == 0
    n2 = seq_len // DFT_RADIX
    m2 = plan.s_pad // DFT_RADIX
    c, s = _dft_angle_tables(n2, m2, m2)
    idx = jnp.arange(m2, dtype=jnp.int32)
    valid = (idx[:, None] < n2) & (idx[None, :] < n2)
    zero = jnp.zeros((), jnp.float32)
    c = jnp.where(valid, c, zero)
    s = jnp.where(valid, s, zero)
    g = jnp.stack([jnp.stack([c, s]), jnp.stack([-s, c])]).astype(jnp.bfloat16)
    k = (jnp.arange(m2, dtype=jnp.int32)[None, :, None]
         + n2 * jnp.arange(DFT_RADIX, dtype=jnp.int32)[:, None, None])
    r = (k * jnp.arange(DFT_RADIX, dtype=jnp.int32)[None, None, :]) % seq_len
    ang = r.astype(jnp.float32) * (2.0 * math.pi / seq_len)
    tw = jnp.concatenate([jnp.cos(ang), jnp.sin(ang)], axis=-1) * (1.0 / math.sqrt(seq_len))
    return g, tw


def _layer_weights(l, w_in, w_uq, w_ukv, w_o):
    bf = jnp.bfloat16
    w = w_in[l]
    s = np.cumsum([0, F_WIDTH, Q_LORA, KV_LORA, QK_ROPE, DIFF_WIDTH, DIFF_WIDTH, DIFF_WIDTH,
                   GATE_WIDTH])
    uf, cq, ckv, kr, qd, kd, vd, gate = (w[:, s[i]:s[i + 1]] for i in range(8))
    half = QK_ROPE // 2
    kr_sw = jnp.concatenate([kr[:, half:], kr[:, :half]], axis=1)
    w_in_r = jnp.concatenate([uf, cq, ckv, kr, kr_sw, qd, kd, vd, gate], axis=1).astype(bf)
    wq = w_uq[l].reshape(Q_LORA, MLA_HEADS, QK_NOPE + QK_ROPE)
    rope = wq[..., QK_NOPE:]
    rope_sw = jnp.concatenate([rope[..., half:], rope[..., :half]], axis=-1)
    wq = jnp.concatenate([wq, rope_sw], axis=-1).transpose(1, 0, 2).astype(bf)
    wkv = w_ukv[l].reshape(KV_LORA, MLA_HEADS, QK_NOPE + V_HEAD).transpose(1, 0, 2).astype(bf)
    return w_in_r, wq, wkv, w_o[l].astype(bf)


def _encode(x, plan, meta_tokens, bias_tiles, final_norm, norm_w, q_norm, kv_norm, diff_norm,
            lam_vecs, pq, layer_w):
    batch, seq_len, s_pad = plan.batch, plan.seq_len, plan.s_pad
    rows = batch * s_pad
    meta = jnp.broadcast_to(meta_tokens[None], (batch, N_META, D_MODEL))
    pad = jnp.zeros((batch, s_pad - seq_len, D_MODEL), x.dtype)
    h = jnp.concatenate([meta, x, pad], axis=1).reshape(rows, D_MODEL)

    t1 = _rope_table(s_pad)
    dft_tables = _seq_dft_tables(plan)
    for l in range(DEPTH):
        w_in_r, wq, wkv, wo = layer_w[l]
        lam_init = 0.8 - 0.6 * math.exp(-0.3 * l)
        ab, cq, ckv, kr, qd, kd, vd, sg = _in_proj(
            h, norm_w[l][None], w_in_r, pq[l], q_norm[l][None], kv_norm[l][None], t1, plan)
        q, k, v = _mla_up(cq, ckv, kr, t1, wq, wkv, plan)
        yf = _seq_dft(dft_tables, ab, plan)
        ym = _mla_attn(q, k, v, plan)
        yd = _diff_attn(qd, kd, vd, bias_tiles, lam_vecs[l], diff_norm[l][None], plan, lam_init)
        h = _out_proj(h, yf, ym, yd, sg, wo, final_norm[None], plan, final=(l == DEPTH - 1))
    return h.reshape(batch, s_pad, D_MODEL)[:, N_META:seq_len]


def kernel(x_prompt, x_sample, meta_tokens, rel_bias, final_norm, norm_w, w_in, w_fmix, q_norm,
           w_uq, kv_norm, w_ukv, lam_q1, lam_k1, lam_q2, lam_k2, diff_norm, w_o):
    cc, sc = _dft_angle_tables(F_GROUP_DIM, F_GROUP_DIM, F_GROUP_DIM)
    chan_cs = jnp.stack([cc, -sc]) * (1.0 / math.sqrt(F_GROUP_DIM))
    pq = _fold_fmix(chan_cs, w_fmix).astype(jnp.bfloat16)
    lam_vecs = jnp.stack([lam_q1, lam_k1, lam_q2, lam_k2], axis=1)
    layer_w = [_layer_weights(l, w_in, w_uq, w_ukv, w_o) for l in range(DEPTH)]
    outs = []
    bias_by_tile = {}
    for x in (x_prompt, x_sample):
        plan = _plan(x.shape[0], x.shape[1])
        if plan.tile not in bias_by_tile:
            bias_by_tile[plan.tile] = _bias_tiles(rel_bias, _bucket_tiles(plan.tile))
        outs.append(_encode(
            x, plan, meta_tokens, bias_by_tile[plan.tile], final_norm, norm_w, q_norm, kv_norm,
            diff_norm, lam_vecs, pq, layer_w))
    return tuple(outs)
```

```python
import functools
import math
from typing import NamedTuple

import jax
import jax.numpy as jnp
import numpy as np
from jax import lax
from jax.experimental import pallas as pl
from jax.experimental.pallas import tpu as pltpu

D_MODEL = 2048
DEPTH = 2
N_META = 16
F_GROUPS = 4
F_GROUP_DIM = 128
F_WIDTH = F_GROUPS * F_GROUP_DIM
MLA_HEADS = 8
Q_LORA = 768
KV_LORA = 512
QK_NOPE = 128
QK_ROPE = 64
V_HEAD = 128
MLA_WIDTH = MLA_HEADS * V_HEAD
ROPE_THETA = 10000.0
DIFF_HEADS = 4
DIFF_QK = 64
DIFF_V = 2 * DIFF_QK
DIFF_WIDTH = DIFF_HEADS * DIFF_V
REL_BUCKETS = 32
REL_MAX_DIST = 128
NORM_EPS = 1e-6
DIFF_NORM_EPS = 1e-5
GATE_WIDTH = D_MODEL

LOG2E = math.log2(math.e)
MLA_QSCALE = LOG2E / math.sqrt(QK_NOPE + QK_ROPE)
DIFF_QSCALE = LOG2E / math.sqrt(DIFF_QK)
MASK_VALUE = -1e30

LANES = 128
VMEM_BYTES_V7X = 64 * 1024 * 1024

C_UF = 0
C_CQ = C_UF + F_WIDTH
C_CKV = C_CQ + Q_LORA
C_KR = C_CKV + KV_LORA
C_QD = C_KR + 2 * QK_ROPE
C_KD = C_QD + DIFF_WIDTH
C_VD = C_KD + DIFF_WIDTH
C_GATE = C_VD + DIFF_WIDTH
IN_COLS = C_GATE + GATE_WIDTH

MAX_ROW_TILE = 384
MAX_KEY_CHUNK = 1408
DFT_RADIX = 16
DFT_COL_TILE = 1024
DFT_ROW_TILE = 88
BIAS_SIDE = 2


class SeqPlan(NamedTuple):
    batch: int
    seq_len: int
    s_pad: int
    tile: int
    chunk: int


def _round_up(n, m):
    return -(-n // m) * m


def _largest_tile(s_pad, limit):
    n = s_pad // LANES
    return LANES * max(d for d in range(1, n + 1) if n % d == 0 and d * LANES <= limit)


def _plan(batch, s_real):
    seq_len = s_real + N_META
    tile = min(MAX_ROW_TILE, _round_up(seq_len, LANES))
    s_pad = _round_up(seq_len, tile)
    return SeqPlan(batch, seq_len, s_pad, tile, _largest_tile(s_pad, MAX_KEY_CHUNK))


def _vmem_limit(nbytes):
    return int(min(nbytes, VMEM_BYTES_V7X - (8 << 20)))


def _rms(x, g, eps):
    ms = jnp.mean(x * x, axis=-1, keepdims=True)
    return x * lax.rsqrt(ms + eps) * g


def _rope_fold(v, t1):
    w = v * t1
    return w + pltpu.roll(w, QK_ROPE, 1)


def _fold_fmix_kernel(cs_ref, w_ref, o_ref):
    w = w_ref[0, 0]
    o_ref[0, 0, :, :F_GROUP_DIM] = jnp.dot(
        cs_ref[0], w, preferred_element_type=jnp.float32,
        precision=lax.Precision.HIGHEST)
    o_ref[0, 0, :, F_GROUP_DIM:] = jnp.dot(
        cs_ref[1], w, preferred_element_type=jnp.float32,
        precision=lax.Precision.HIGHEST)


def _fold_fmix(chan_cs, w_fmix):
    c = F_GROUP_DIM
    return pl.pallas_call(
        _fold_fmix_kernel,
        out_shape=jax.ShapeDtypeStruct((DEPTH, F_GROUPS, c, 2 * c), jnp.float32),
        grid=(DEPTH, F_GROUPS),
        in_specs=[pl.BlockSpec((2, c, c), lambda l, g: (0, 0, 0)),
                  pl.BlockSpec((1, 1, c, c), lambda l, g: (l, g, 0, 0))],
        out_specs=pl.BlockSpec((1, 1, c, 2 * c), lambda l, g: (l, g, 0, 0)),
        name="fold_fmix",
    )(chan_cs, w_fmix)


def _bias_tiles_kernel(tab_ref, bucket_ref, o_ref):
    h = pl.program_id(1)
    bucket = bucket_ref[0]
    acc = jnp.zeros(bucket.shape, jnp.float32)
    for b in range(REL_BUCKETS):
        acc = jnp.where(bucket == b, tab_ref[b * DIFF_HEADS + h], acc)
    o_ref[0, 0] = acc * LOG2E


def _bias_tiles(rel_bias, bucket_tiles):
    nd, t, w = bucket_tiles.shape
    return pl.pallas_call(
        _bias_tiles_kernel,
        out_shape=jax.ShapeDtypeStruct((DIFF_HEADS, nd, t, w), jnp.float32),
        grid=(nd, DIFF_HEADS),
        in_specs=[pl.BlockSpec(memory_space=pltpu.SMEM),
                  pl.BlockSpec((1, t, w), lambda d, h: (d, 0, 0))],
        out_specs=pl.BlockSpec((1, 1, t, w), lambda d, h: (h, d, 0, 0)),
        name="bias_tiles",
    )(rel_bias.reshape(-1), bucket_tiles)


def _in_proj_kernel(x_ref, nw_ref, w_ref, pq_ref, qn_ref, kvn_ref, t1_ref,
                    ab_ref, cq_ref, ckv_ref, kr_ref, qd_ref, kd_ref, vd_ref, sg_ref):
    x = x_ref[...]
    xn = _rms(x, nw_ref[...], NORM_EPS).astype(jnp.bfloat16)

    def proj(c0, width):
        return jnp.dot(xn, w_ref[:, c0:c0 + width], preferred_element_type=jnp.float32)

    uf = proj(C_UF, F_WIDTH).astype(jnp.bfloat16)
    c = F_GROUP_DIM
    for g in range(F_GROUPS):
        ab = jnp.dot(uf[:, g * c:(g + 1) * c], pq_ref[g], preferred_element_type=jnp.float32)
        ab_ref[0, 0, :, g * c:(g + 1) * c] = ab[:, :c].astype(jnp.bfloat16)
        ab_ref[0, 1, :, g * c:(g + 1) * c] = ab[:, c:].astype(jnp.bfloat16)

    cq_ref[...] = _rms(proj(C_CQ, Q_LORA), qn_ref[...], NORM_EPS).astype(jnp.bfloat16)
    ckv_ref[...] = _rms(proj(C_CKV, KV_LORA), kvn_ref[...], NORM_EPS).astype(jnp.bfloat16)
    kr_ref[...] = _rope_fold(proj(C_KR, 2 * QK_ROPE), t1_ref[...]).astype(jnp.bfloat16)
    qd_ref[...] = (proj(C_QD, DIFF_WIDTH) * DIFF_QSCALE).astype(jnp.bfloat16)
    kd_ref[...] = proj(C_KD, DIFF_WIDTH).astype(jnp.bfloat16)
    vd_ref[...] = proj(C_VD, DIFF_WIDTH).astype(jnp.bfloat16)
    step = 512
    for c0 in range(0, GATE_WIDTH, step):
        gate = proj(C_GATE + c0, step)
        sg_ref[:, c0:c0 + step] = (gate * jax.nn.sigmoid(gate)).astype(sg_ref.dtype)


def _in_proj(h, norm_w, w_in, pq, q_norm, kv_norm, t1, plan):
    rows = h.shape[0]
    tm = plan.tile
    nt = plan.s_pad // tm
    row = lambda width: pl.BlockSpec((tm, width), lambda i: (i, 0))
    const = lambda shape: pl.BlockSpec(shape, lambda i: (0,) * len(shape),
                                       pipeline_mode=pl.Buffered(1))
    bf = jnp.bfloat16
    out_shape = (
        jax.ShapeDtypeStruct((plan.batch, 2, plan.s_pad, F_WIDTH), bf),
        jax.ShapeDtypeStruct((rows, Q_LORA), bf),
        jax.ShapeDtypeStruct((rows, KV_LORA), bf),
        jax.ShapeDtypeStruct((rows, 2 * QK_ROPE), bf),
        jax.ShapeDtypeStruct((rows, DIFF_WIDTH), bf),
        jax.ShapeDtypeStruct((rows, DIFF_WIDTH), bf),
        jax.ShapeDtypeStruct((rows, DIFF_WIDTH), bf),
        jax.ShapeDtypeStruct((rows, GATE_WIDTH), bf),
    )
    out_specs = (
        pl.BlockSpec((1, 2, tm, F_WIDTH), lambda i: (i // nt, 0, i % nt, 0)),
        row(Q_LORA), row(KV_LORA), row(2 * QK_ROPE),
        row(DIFF_WIDTH), row(DIFF_WIDTH), row(DIFF_WIDTH), row(GATE_WIDTH),
    )
    in_specs = [
        row(D_MODEL),
        const((1, D_MODEL)),
        const((D_MODEL, IN_COLS)),
        const((F_GROUPS, F_GROUP_DIM, 2 * F_GROUP_DIM)),
        const((1, Q_LORA)),
        const((1, KV_LORA)),
        pl.BlockSpec((tm, 2 * QK_ROPE), lambda i: (i % nt, 0)),
    ]
    return pl.pallas_call(
        _in_proj_kernel,
        out_shape=out_shape,
        grid=(rows // tm,),
        in_specs=in_specs,
        out_specs=out_specs,
        compiler_params=pltpu.CompilerParams(
            dimension_semantics=("arbitrary",), vmem_limit_bytes=_vmem_limit(56 << 20)),
        name="in_proj",
    )(h, norm_w, w_in, pq, q_norm, kv_norm, t1)


def _mla_up_kernel(cq_ref, ckv_ref, kr_ref, t1_ref, wq_ref, wkv_ref, q_ref, k_ref, v_ref):
    cq = cq_ref[...]
    ckv = ckv_ref[...]
    t1 = t1_ref[...]
    k_rope = kr_ref[:, :QK_ROPE]
    for h in range(MLA_HEADS):
        r = jnp.dot(cq, wq_ref[h], preferred_element_type=jnp.float32) * MLA_QSCALE
        q_ref[h, :, :QK_NOPE] = r[:, :QK_NOPE].astype(jnp.bfloat16)
        rope = _rope_fold(r[:, QK_NOPE:], t1)
        q_ref[h, :, QK_NOPE:] = rope[:, :QK_ROPE].astype(jnp.bfloat16)
        kv = jnp.dot(ckv, wkv_ref[h], preferred_element_type=jnp.float32)
        k_ref[h, :, :QK_NOPE] = kv[:, :QK_NOPE].astype(jnp.bfloat16)
        k_ref[h, :, QK_NOPE:] = k_rope
        v_ref[h] = kv[:, QK_NOPE:].astype(jnp.bfloat16)


def _mla_up(cq, ckv, kr, t1, wq, wkv, plan):
    rows = cq.shape[0]
    tm = plan.tile
    nt = plan.s_pad // tm
    dqk = QK_NOPE + QK_ROPE
    bf = jnp.bfloat16
    row = lambda width: pl.BlockSpec((tm, width), lambda i: (i, 0))
    heads = lambda width: pl.BlockSpec((MLA_HEADS, tm, width), lambda i: (0, i, 0))
    const = lambda shape: pl.BlockSpec(shape, lambda i: (0,) * len(shape))
    return pl.pallas_call(
        _mla_up_kernel,
        out_shape=(jax.ShapeDtypeStruct((MLA_HEADS, rows, dqk), bf),
                   jax.ShapeDtypeStruct((MLA_HEADS, rows, dqk), bf),
                   jax.ShapeDtypeStruct((MLA_HEADS, rows, V_HEAD), bf)),
        grid=(rows // tm,),
        in_specs=[row(Q_LORA), row(KV_LORA), row(2 * QK_ROPE),
                  pl.BlockSpec((tm, 2 * QK_ROPE), lambda i: (i % nt, 0)),
                  const((MLA_HEADS, Q_LORA, QK_NOPE + 2 * QK_ROPE)),
                  const((MLA_HEADS, KV_LORA, QK_NOPE + V_HEAD))],
        out_specs=(heads(dqk), heads(dqk), heads(V_HEAD)),
        compiler_params=pltpu.CompilerParams(dimension_semantics=("arbitrary",)),
        name="mla_up",
    )(cq, ckv, kr, t1, wq, wkv)


def _dft_inner_kernel(g_ref, x_ref, o_ref):
    a = x_ref[0, 0]
    b = x_ref[0, 1]
    dot = functools.partial(jnp.dot, preferred_element_type=jnp.float32)
    o_ref[0, 0] = dot(g_ref[0, 0], a) + dot(g_ref[0, 1], b)
    o_ref[0, 1] = dot(g_ref[1, 0], a) + dot(g_ref[1, 1], b)


def _dft_inner(g, x):
    batch, _, m2, width = x.shape
    tn = DFT_COL_TILE
    return pl.pallas_call(
        _dft_inner_kernel,
        out_shape=jax.ShapeDtypeStruct((batch, 2, m2, width), jnp.float32),
        grid=(batch, width // tn),
        in_specs=[pl.BlockSpec((2, 2, m2, m2), lambda b, j: (0, 0, 0, 0)),
                  pl.BlockSpec((1, 2, m2, tn), lambda b, j: (b, 0, 0, j))],
        out_specs=pl.BlockSpec((1, 2, m2, tn), lambda b, j: (b, 0, 0, j)),
        compiler_params=pltpu.CompilerParams(
            dimension_semantics=("arbitrary", "arbitrary"),
            vmem_limit_bytes=_vmem_limit(40 << 20)),
        name="dft_inner",
    )(g, x)


def _dft_outer_kernel(tw_ref, y_ref, o_ref):
    def body(k1, carry):
        tw = tw_ref[k1]
        acc = jnp.zeros(o_ref.shape[2:], jnp.float32)
        for n1 in range(DFT_RADIX):
            cols = slice(n1 * F_WIDTH, (n1 + 1) * F_WIDTH)
            acc = acc + tw[:, n1:n1 + 1] * y_ref[0, 0, :, cols]
            acc = acc + tw[:, DFT_RADIX + n1:DFT_RADIX + n1 + 1] * y_ref[0, 1, :, cols]
        o_ref[0, k1] = acc
        return carry

    lax.fori_loop(0, DFT_RADIX, body, 0)


def _dft_outer(tw, y):
    batch, _, m2, width = y.shape
    t2 = max(d for d in range(8, m2 + 1, 8) if m2 % d == 0 and d <= DFT_ROW_TILE)
    return pl.pallas_call(
        _dft_outer_kernel,
        out_shape=jax.ShapeDtypeStruct((batch, DFT_RADIX, m2, F_WIDTH), jnp.float32),
        grid=(batch, m2 // t2),
        in_specs=[pl.BlockSpec((DFT_RADIX, t2, 2 * DFT_RADIX), lambda b, j: (0, j, 0)),
                  pl.BlockSpec((1, 2, t2, width), lambda b, j: (b, 0, j, 0))],
        out_specs=pl.BlockSpec((1, DFT_RADIX, t2, F_WIDTH), lambda b, j: (b, 0, j, 0)),
        compiler_params=pltpu.CompilerParams(
            dimension_semantics=("arbitrary", "arbitrary"),
            vmem_limit_bytes=_vmem_limit(40 << 20)),
        name="dft_outer",
    )(tw, y)


def _seq_dft(tables, ab, plan):
    g, tw = tables
    batch, s_pad, n2 = plan.batch, plan.s_pad, plan.seq_len // DFT_RADIX
    m2 = s_pad // DFT_RADIX
    y = _dft_inner(g, ab.reshape(batch, 2, m2, DFT_RADIX * F_WIDTH))
    out = _dft_outer(tw, y)[:, :, :n2].reshape(batch, DFT_RADIX * n2, F_WIDTH)
    out = jnp.pad(out, ((0, 0), (0, s_pad - DFT_RADIX * n2), (0, 0)))
    return out.reshape(batch * s_pad, F_WIDTH)


def _key_chunks(plan):
    k_end = _round_up(plan.seq_len, LANES)
    return [(c0, min(plan.chunk, k_end - c0)) for c0 in range(0, k_end, plan.chunk)]


def _mask_pad_keys(s, c0, seq_len):
    width = s.shape[-1]
    n_valid = seq_len - c0
    if n_valid >= width:
        return s
    a = width - LANES
    assert n_valid > a
    lane = lax.broadcasted_iota(jnp.int32, (s.shape[0], LANES), 1)
    tail = jnp.where(lane < n_valid - a, s[:, a:], MASK_VALUE)
    return jnp.concatenate([s[:, :a], tail], axis=1) if a else tail


def _softmax_step(s, vx, state):
    if state is None:
        m_new = jnp.max(s, axis=-1, keepdims=True)
        p = jnp.exp2((s - m_new).astype(jnp.bfloat16))
        return m_new, jnp.dot(p, vx, preferred_element_type=jnp.float32)
    m, acc = state
    m_new = jnp.maximum(m, jnp.max(s, axis=-1, keepdims=True))
    p = jnp.exp2((s - m_new).astype(jnp.bfloat16))
    alpha = jnp.exp2(m - m_new)
    return m_new, alpha * acc + jnp.dot(p, vx, preferred_element_type=jnp.float32)


def _normalized(acc):
    return acc[:, :V_HEAD] / acc[:, V_HEAD:V_HEAD + 1]


def _fill_value_ext(vx_ref, v):
    vx_ref[:, :V_HEAD] = v
    lane = lax.broadcasted_iota(jnp.int32, v.shape, 1)
    vx_ref[:, V_HEAD:] = jnp.where(lane == 0, 1.0, 0.0).astype(vx_ref.dtype)


def _qk(q, k):
    return lax.dot_general(q, k, (((1,), (1,)), ((), ())), preferred_element_type=jnp.float32)


def _tile_rows(tile, tq):
    return pl.ds(pl.multiple_of(tile * tq, tq), tq)


def _attn_pipeline(n_tiles, head, tail):
    def body(i, carry):
        tail(jnp.maximum(i - 1, 0))
        head(i)
        return carry

    lax.fori_loop(0, n_tiles, body, 0)
    tail(n_tiles - 1)


def _init_attn_state(s_last, m_sc, acc_sc):
    s_last[...] = jnp.zeros(s_last.shape, jnp.float32)
    m_sc[...] = jnp.zeros(m_sc.shape, jnp.float32)
    acc_sc[...] = jnp.ones(acc_sc.shape, jnp.float32)


def _store_attn_state(state, m_sc, acc_sc):
    if state is None:
        m_sc[...] = jnp.full(m_sc.shape, MASK_VALUE, jnp.float32)
        acc_sc[...] = jnp.zeros(acc_sc.shape, jnp.float32)
    else:
        m_sc[...], acc_sc[...] = state


def _mla_attn_kernel(q_ref, k_ref, v_ref, o_ref, vx_ref, s_last, m_sc, acc_sc,
                     *, chunks, seq_len, tq):
    c_last, w_last = chunks[-1]
    _init_attn_state(s_last, m_sc, acc_sc)
    _fill_value_ext(vx_ref, v_ref[0])

    def head(tile):
        q = q_ref[0, _tile_rows(tile, tq), :]
        state = None
        for c0, width in chunks[:-1]:
            s = _qk(q, k_ref[0, c0:c0 + width, :])
            state = _softmax_step(s, vx_ref[c0:c0 + width, :], state)
        s = _qk(q, k_ref[0, c_last:c_last + w_last, :])
        s_last[...] = _mask_pad_keys(s, c_last, seq_len)
        _store_attn_state(state, m_sc, acc_sc)

    def tail(tile):
        _, acc = _softmax_step(s_last[...], vx_ref[c_last:c_last + w_last, :],
                               (m_sc[...], acc_sc[...]))
        o_ref[_tile_rows(tile, tq), :] = _normalized(acc).astype(o_ref.dtype)

    _attn_pipeline(q_ref.shape[1] // tq, head, tail)


def _mla_attn(q, k, v, plan):
    tq = plan.tile
    dqk = QK_NOPE + QK_ROPE
    chunks = _key_chunks(plan)
    per_head = lambda width: pl.BlockSpec((1, plan.s_pad, width), lambda b, h: (h, b, 0))
    return pl.pallas_call(
        functools.partial(_mla_attn_kernel, chunks=chunks, seq_len=plan.seq_len, tq=tq),
        out_shape=jax.ShapeDtypeStruct((plan.batch * plan.s_pad, MLA_WIDTH), jnp.bfloat16),
        grid=(plan.batch, MLA_HEADS),
        in_specs=[per_head(dqk), per_head(dqk), per_head(V_HEAD)],
        out_specs=pl.BlockSpec((plan.s_pad, V_HEAD), lambda b, h: (b, h)),
        scratch_shapes=[pltpu.VMEM((plan.s_pad, 2 * V_HEAD), jnp.bfloat16),
                        pltpu.VMEM((tq, chunks[-1][1]), jnp.float32),
                        pltpu.VMEM((tq, 1), jnp.float32),
                        pltpu.VMEM((tq, 2 * V_HEAD), jnp.float32)],
        compiler_params=pltpu.CompilerParams(
            dimension_semantics=("arbitrary", "arbitrary"),
            vmem_limit_bytes=_vmem_limit(56 << 20)),
        name="mla_attn",
    )(q, k, v)


def _diff_attn_kernel(q_ref, k_ref, v_ref, bt_ref, lam_ref, g_ref, o_ref,
                      vx_ref, s_last, m_sc, acc_sc, *, chunks, seq_len, tq, lam_init):
    q_tiles = tq // LANES
    c_last, w_last = chunks[-1]
    _init_attn_state(s_last, m_sc, acc_sc)
    _fill_value_ext(vx_ref, v_ref[...])
    lv = lam_ref[...]
    lam = (jnp.exp(jnp.sum(lv[0:1] * lv[1:2], axis=-1, keepdims=True))
           - jnp.exp(jnp.sum(lv[2:3] * lv[3:4], axis=-1, keepdims=True)) + lam_init)

    def scores(q2x, tile, c0, width):
        s = _qk(q2x, k_ref[c0:c0 + width, :])
        parts = []
        for g in range(width // LANES):
            o = (c0 // LANES + g) - tile * q_tiles
            bias = bt_ref[0, jnp.clip(o, -BIAS_SIDE, q_tiles + BIAS_SIDE - 1) + BIAS_SIDE]
            sub = s[:, g * LANES:(g + 1) * LANES]
            parts.append(jnp.concatenate([sub[:tq] + bias, sub[tq:] + bias], axis=0))
        return jnp.concatenate(parts, axis=1)

    def head(tile):
        qq = q_ref[_tile_rows(tile, tq), :].astype(jnp.float32)
        lane = lax.broadcasted_iota(jnp.int32, qq.shape, 1)
        zero = jnp.zeros_like(qq)
        q2x = jnp.concatenate([jnp.where(lane < DIFF_QK, qq, zero),
                               jnp.where(lane >= DIFF_QK, qq, zero)], axis=0).astype(jnp.bfloat16)
        state = None
        for c0, width in chunks[:-1]:
            state = _softmax_step(scores(q2x, tile, c0, width), vx_ref[c0:c0 + width, :], state)
        s_last[...] = _mask_pad_keys(scores(q2x, tile, c_last, w_last), c_last, seq_len)
        _store_attn_state(state, m_sc, acc_sc)

    def tail(tile):
        _, acc = _softmax_step(s_last[...], vx_ref[c_last:c_last + w_last, :],
                               (m_sc[...], acc_sc[...]))
        o = _normalized(acc)
        od = o[:tq] - lam * o[tq:]
        o_ref[_tile_rows(tile, tq), :] = (
            _rms(od, g_ref[...], DIFF_NORM_EPS) * (1.0 - lam_init)).astype(o_ref.dtype)

    _attn_pipeline(q_ref.shape[0] // tq, head, tail)


def _diff_attn(qd, kd, vd, bias_tiles, lam_vec, diff_norm, plan, lam_init):
    tq = plan.tile
    nd = bias_tiles.shape[1]
    chunks = _key_chunks(plan)
    per_head = pl.BlockSpec((plan.s_pad, DIFF_V), lambda b, h: (b, h))
    return pl.pallas_call(
        functools.partial(_diff_attn_kernel, chunks=chunks, seq_len=plan.seq_len, tq=tq,
                          lam_init=lam_init),
        out_shape=jax.ShapeDtypeStruct((plan.batch * plan.s_pad, DIFF_WIDTH), jnp.bfloat16),
        grid=(plan.batch, DIFF_HEADS),
        in_specs=[per_head, per_head, per_head,
                  pl.BlockSpec((1, nd, tq, LANES), lambda b, h: (h, 0, 0, 0)),
                  pl.BlockSpec((4, DIFF_QK), lambda b, h: (0, 0)),
                  pl.BlockSpec((1, DIFF_V), lambda b, h: (0, 0))],
        out_specs=per_head,
        scratch_shapes=[pltpu.VMEM((plan.s_pad, 2 * DIFF_V), jnp.bfloat16),
                        pltpu.VMEM((2 * tq, chunks[-1][1]), jnp.float32),
                        pltpu.VMEM((2 * tq, 1), jnp.float32),
                        pltpu.VMEM((2 * tq, 2 * DIFF_V), jnp.float32)],
        compiler_params=pltpu.CompilerParams(
            dimension_semantics=("arbitrary", "arbitrary"),
            vmem_limit_bytes=_vmem_limit(56 << 20)),
        name="diff_attn",
    )(qd, kd, vd, bias_tiles, lam_vec, diff_norm)


def _out_proj_kernel(x_ref, yf_ref, ym_ref, yd_ref, sg_ref, wo_ref, fn_ref, o_ref, *, final):
    c1 = F_WIDTH
    c2 = F_WIDTH + MLA_WIDTH

    def seg(y, c0, c1_):
        yg = (y.astype(jnp.float32) * sg_ref[:, c0:c1_].astype(jnp.float32)).astype(jnp.bfloat16)
        return jnp.dot(yg, wo_ref[c0:c1_, :], preferred_element_type=jnp.float32)

    o = (x_ref[...] + seg(yf_ref[...], 0, c1) + seg(ym_ref[...], c1, c2)
         + seg(yd_ref[...], c2, D_MODEL))
    if final:
        o = _rms(o, fn_ref[...], NORM_EPS)
    o_ref[...] = o


def _out_proj(h, yf, ym, yd, sg, wo, final_norm, plan, final):
    rows = h.shape[0]
    tm = plan.tile
    row = lambda width: pl.BlockSpec((tm, width), lambda i: (i, 0))
    const = lambda shape: pl.BlockSpec(shape, lambda i: (0,) * len(shape),
                                       pipeline_mode=pl.Buffered(1))
    return pl.pallas_call(
        functools.partial(_out_proj_kernel, final=final),
        out_shape=jax.ShapeDtypeStruct((rows, D_MODEL), jnp.float32),
        grid=(rows // tm,),
        in_specs=[row(D_MODEL), row(F_WIDTH), row(MLA_WIDTH), row(DIFF_WIDTH), row(GATE_WIDTH),
                  const((D_MODEL, D_MODEL)), const((1, D_MODEL))],
        out_specs=row(D_MODEL),
        compiler_params=pltpu.CompilerParams(
            dimension_semantics=("arbitrary",), vmem_limit_bytes=_vmem_limit(48 << 20)),
        name="out_proj",
    )(h, yf, ym, yd, sg, wo, final_norm)


def _t5_bucket(rel):
    nb = REL_BUCKETS // 2
    max_exact = nb // 2
    ret = (rel > 0).astype(jnp.int32) * nb
    n = jnp.abs(rel)
    nf = jnp.maximum(n, 1).astype(jnp.float32)
    large = max_exact + (jnp.log(nf / max_exact) / math.log(REL_MAX_DIST / max_exact)
                         * (nb - max_exact)).astype(jnp.int32)
    large = jnp.minimum(large, nb - 1)
    return ret + jnp.where(n < max_exact, n, large)


def _bucket_tiles(tq):
    assert LANES >= REL_MAX_DIST
    o = jnp.arange(-BIAS_SIDE, tq // LANES + BIAS_SIDE, dtype=jnp.int32)[:, None, None] * LANES
    row = jnp.arange(tq, dtype=jnp.int32)[None, :, None]
    col = jnp.arange(LANES, dtype=jnp.int32)[None, None, :]
    return _t5_bucket(o + col - row)


def _rope_table(s_pad):
    pos = jnp.arange(s_pad, dtype=jnp.float32)
    inv_freq = ROPE_THETA ** (-jnp.arange(0, QK_ROPE, 2, dtype=jnp.float32) / QK_ROPE)
    ang = pos[:, None] * inv_freq[None, :]
    cos, sin = jnp.cos(ang), jnp.sin(ang)
    return jnp.concatenate([cos, cos, -sin, sin], axis=-1)


def _dft_angle_tables(n, rows, cols):
    r = (jnp.arange(rows, dtype=jnp.int32)[:, None] * jnp.arange(cols, dtype=jnp.int32)[None, :]) % n
    ang = r.astype(jnp.float32) * (2.0 * math.pi / n)
    return jnp.cos(ang), jnp.sin(ang)


def _seq_dft_tables(plan):
    seq_len = plan.seq_len
    assert seq_len % DFT_RADIX == 0 and plan.s_pad % DFT_RADIX == 0
    n2 = seq_len // DFT_RADIX
    m2 = plan.s_pad // DFT_RADIX
    c, s = _dft_angle_tables(n2, m2, m2)
    idx = jnp.arange(m2, dtype=jnp.int32)
    valid = (idx[:, None] < n2) & (idx[None, :] < n2)
    zero = jnp.zeros((), jnp.float32)
    c = jnp.where(valid, c, zero)
    s = jnp.where(valid, s, zero)
    g = jnp.stack([jnp.stack([c, s]), jnp.stack([-s, c])]).astype(jnp.bfloat16)
    k = (jnp.arange(m2, dtype=jnp.int32)[None, :, None]
         + n2 * jnp.arange(DFT_RADIX, dtype=jnp.int32)[:, None, None])
    r = (k * jnp.arange(DFT_RADIX, dtype=jnp.int32)[None, None, :]) % seq_len
    ang = r.astype(jnp.float32) * (2.0 * math.pi / seq_len)
    tw = jnp.concatenate([jnp.cos(ang), jnp.sin(ang)], axis=-1) * (1.0 / math.sqrt(seq_len))
    return g, tw


def _layer_weights(l, w_in, w_uq, w_ukv, w_o):
    bf = jnp.bfloat16
    w = w_in[l]
    s = np.cumsum([0, F_WIDTH, Q_LORA, KV_LORA, QK_ROPE, DIFF_WIDTH, DIFF_WIDTH, DIFF_WIDTH,
                   GATE_WIDTH])
    uf, cq, ckv, kr, qd, kd, vd, gate = (w[:, s[i]:s[i + 1]] for i in range(8))
    half = QK_ROPE // 2
    kr_sw = jnp.concatenate([kr[:, half:], kr[:, :half]], axis=1)
    w_in_r = jnp.concatenate([uf, cq, ckv, kr, kr_sw, qd, kd, vd, gate], axis=1).astype(bf)
    wq = w_uq[l].reshape(Q_LORA, MLA_HEADS, QK_NOPE + QK_ROPE)
    rope = wq[..., QK_NOPE:]
    rope_sw = jnp.concatenate([rope[..., half:], rope[..., :half]], axis=-1)
    wq = jnp.concatenate([wq, rope_sw], axis=-1).transpose(1, 0, 2).astype(bf)
    wkv = w_ukv[l].reshape(KV_LORA, MLA_HEADS, QK_NOPE + V_HEAD).transpose(1, 0, 2).astype(bf)
    return w_in_r, wq, wkv, w_o[l].astype(bf)


def _encode(x, plan, meta_tokens, bias_tiles, final_norm, norm_w, q_norm, kv_norm, diff_norm,
            lam_vecs, pq, layer_w):
    batch, seq_len, s_pad = plan.batch, plan.seq_len, plan.s_pad
    rows = batch * s_pad
    meta = jnp.broadcast_to(meta_tokens[None], (batch, N_META, D_MODEL))
    pad = jnp.zeros((batch, s_pad - seq_len, D_MODEL), x.dtype)
    h = jnp.concatenate([meta, x, pad], axis=1).reshape(rows, D_MODEL)

    t1 = _rope_table(s_pad)
    dft_tables = _seq_dft_tables(plan)
    for l in range(DEPTH):
        w_in_r, wq, wkv, wo = layer_w[l]
        lam_init = 0.8 - 0.6 * math.exp(-0.3 * l)
        ab, cq, ckv, kr, qd, kd, vd, sg = _in_proj(
            h, norm_w[l][None], w_in_r, pq[l], q_norm[l][None], kv_norm[l][None], t1, plan)
        q, k, v = _mla_up(cq, ckv, kr, t1, wq, wkv, plan)
        yf = _seq_dft(dft_tables, ab, plan)
        ym = _mla_attn(q, k, v, plan)
        yd = _diff_attn(qd, kd, vd, bias_tiles, lam_vecs[l], diff_norm[l][None], plan, lam_init)
        h = _out_proj(h, yf, ym, yd, sg, wo, final_norm[None], plan, final=(l == DEPTH - 1))
    return h.reshape(batch, s_pad, D_MODEL)[:, N_META:seq_len]


def kernel(x_prompt, x_sample, meta_tokens, rel_bias, final_norm, norm_w, w_in, w_fmix, q_norm,
           w_uq, kv_norm, w_ukv, lam_q1, lam_k1, lam_q2, lam_k2, diff_norm, w_o):
    cc, sc = _dft_angle_tables(F_GROUP_DIM, F_GROUP_DIM, F_GROUP_DIM)
    chan_cs = jnp.stack([cc, -sc]) * (1.0 / math.sqrt(F_GROUP_DIM))
    pq = _fold_fmix(chan_cs, w_fmix).astype(jnp.bfloat16)
    lam_vecs = jnp.stack([lam_q1, lam_k1, lam_q2, lam_k2], axis=1)
    layer_w = [_layer_weights(l, w_in, w_uq, w_ukv, w_o) for l in range(DEPTH)]
    outs = []
    bias_by_tile = {}
    for x in (x_prompt, x_sample):
        plan = _plan(x.shape[0], x.shape[1])
        if plan.tile not in bias_by_tile:
            bias_by_tile[plan.tile] = _bias_tiles(rel_bias, _bucket_tiles(plan.tile))
        outs.append(_encode(
            x, plan, meta_tokens, bias_by_tile[plan.tile], final_norm, norm_w, q_norm, kv_norm,
            diff_norm, lam_vecs, pq, layer_w))
    return tuple(outs)
```

```python
import functools
import math
from typing import NamedTuple

import jax
import jax.numpy as jnp
import numpy as np
from jax import lax
from jax.experimental import pallas as pl
from jax.experimental.pallas import tpu as pltpu

D_MODEL = 2048
DEPTH = 2
N_META = 16
F_GROUPS = 4
F_GROUP_DIM = 128
F_WIDTH = F_GROUPS * F_GROUP_DIM
MLA_HEADS = 8
Q_LORA = 768
KV_LORA = 512
QK_NOPE = 128
QK_ROPE = 64
V_HEAD = 128
MLA_WIDTH = MLA_HEADS * V_HEAD
ROPE_THETA = 10000.0
DIFF_HEADS = 4
DIFF_QK = 64
DIFF_V = 2 * DIFF_QK
DIFF_WIDTH = DIFF_HEADS * DIFF_V
REL_BUCKETS = 32
REL_MAX_DIST = 128
NORM_EPS = 1e-6
DIFF_NORM_EPS = 1e-5
GATE_WIDTH = D_MODEL

LOG2E = math.log2(math.e)
MLA_QSCALE = LOG2E / math.sqrt(QK_NOPE + QK_ROPE)
DIFF_QSCALE = LOG2E / math.sqrt(DIFF_QK)
MASK_VALUE = -1e30

LANES = 128
VMEM_BYTES_V7X = 64 * 1024 * 1024

C_UF = 0
C_CQ = C_UF + F_WIDTH
C_CKV = C_CQ + Q_LORA
C_KR = C_CKV + KV_LORA
C_QD = C_KR + 2 * QK_ROPE
C_KD = C_QD + DIFF_WIDTH
C_VD = C_KD + DIFF_WIDTH
C_GATE = C_VD + DIFF_WIDTH
IN_COLS = C_GATE + GATE_WIDTH

MAX_ROW_TILE = 384
MAX_KEY_CHUNK = 2816
DFT_RADIX = 16
DFT_COL_TILE = 1024
DFT_ROW_TILE = 88
BIAS_SIDE = 2
FINAL_ROW_TILE = 256


class SeqPlan(NamedTuple):
    batch: int
    seq_len: int
    s_pad: int
    tile: int
    chunk: int


def _round_up(n, m):
    return -(-n // m) * m


def _largest_tile(s_pad, limit):
    n = s_pad // LANES
    return LANES * max(d for d in range(1, n + 1) if n % d == 0 and d * LANES <= limit)


def _plan(batch, s_real):
    seq_len = s_real + N_META
    tile = min(MAX_ROW_TILE, _round_up(seq_len, LANES))
    s_pad = _round_up(seq_len, tile)
    return SeqPlan(batch, seq_len, s_pad, tile, _largest_tile(s_pad, MAX_KEY_CHUNK))


def _vmem_limit(nbytes):
    return int(min(nbytes, VMEM_BYTES_V7X - (8 << 20)))


def _rms(x, g, eps):
    ms = jnp.mean(x * x, axis=-1, keepdims=True)
    return x * lax.rsqrt(ms + eps) * g


def _rope_fold(v, t1):
    w = v * t1
    return w + pltpu.roll(w, QK_ROPE, 1)


def _fold_fmix_kernel(cs_ref, w_ref, o_ref):
    w = w_ref[0, 0]
    o_ref[0, 0, :, :F_GROUP_DIM] = jnp.dot(
        cs_ref[0], w, preferred_element_type=jnp.float32,
        precision=lax.Precision.HIGHEST)
    o_ref[0, 0, :, F_GROUP_DIM:] = jnp.dot(
        cs_ref[1], w, preferred_element_type=jnp.float32,
        precision=lax.Precision.HIGHEST)


def _fold_fmix(chan_cs, w_fmix):
    c = F_GROUP_DIM
    return pl.pallas_call(
        _fold_fmix_kernel,
        out_shape=jax.ShapeDtypeStruct((DEPTH, F_GROUPS, c, 2 * c), jnp.float32),
        grid=(DEPTH, F_GROUPS),
        in_specs=[pl.BlockSpec((2, c, c), lambda l, g: (0, 0, 0)),
                  pl.BlockSpec((1, 1, c, c), lambda l, g: (l, g, 0, 0))],
        out_specs=pl.BlockSpec((1, 1, c, 2 * c), lambda l, g: (l, g, 0, 0)),
        name="fold_fmix",
    )(chan_cs, w_fmix)


def _bias_tiles_kernel(tab_ref, bucket_ref, o_ref):
    h = pl.program_id(1)
    bucket = bucket_ref[0]
    acc = jnp.zeros(bucket.shape, jnp.float32)
    for b in range(REL_BUCKETS):
        acc = jnp.where(bucket == b, tab_ref[b * DIFF_HEADS + h], acc)
    o_ref[0, 0] = acc * LOG2E


def _bias_tiles(rel_bias, bucket_tiles):
    nd, t, w = bucket_tiles.shape
    return pl.pallas_call(
        _bias_tiles_kernel,
        out_shape=jax.ShapeDtypeStruct((DIFF_HEADS, nd, t, w), jnp.float32),
        grid=(nd, DIFF_HEADS),
        in_specs=[pl.BlockSpec(memory_space=pltpu.SMEM),
                  pl.BlockSpec((1, t, w), lambda d, h: (d, 0, 0))],
        out_specs=pl.BlockSpec((1, 1, t, w), lambda d, h: (h, d, 0, 0)),
        name="bias_tiles",
    )(rel_bias.reshape(-1), bucket_tiles)


def _in_proj_kernel(x_ref, nw_ref, w_ref, pq_ref, qn_ref, kvn_ref, t1_ref,
                    ab_ref, cq_ref, ckv_ref, kr_ref, qd_ref, kd_ref, vd_ref, sg_ref):
    x = x_ref[...]
    xn = _rms(x, nw_ref[...], NORM_EPS).astype(jnp.bfloat16)

    def proj(c0, width):
        return jnp.dot(xn, w_ref[:, c0:c0 + width], preferred_element_type=jnp.float32)

    uf = proj(C_UF, F_WIDTH).astype(jnp.bfloat16)
    c = F_GROUP_DIM
    for g in range(F_GROUPS):
        ab = jnp.dot(uf[:, g * c:(g + 1) * c], pq_ref[g], preferred_element_type=jnp.float32)
        ab_ref[0, 0, :, g * c:(g + 1) * c] = ab[:, :c].astype(jnp.bfloat16)
        ab_ref[0, 1, :, g * c:(g + 1) * c] = ab[:, c:].astype(jnp.bfloat16)

    cq_ref[...] = _rms(proj(C_CQ, Q_LORA), qn_ref[...], NORM_EPS).astype(jnp.bfloat16)
    ckv_ref[...] = _rms(proj(C_CKV, KV_LORA), kvn_ref[...], NORM_EPS).astype(jnp.bfloat16)
    kr_ref[...] = _rope_fold(proj(C_KR, 2 * QK_ROPE), t1_ref[...]).astype(jnp.bfloat16)
    qd_ref[...] = (proj(C_QD, DIFF_WIDTH) * DIFF_QSCALE).astype(jnp.bfloat16)
    kd_ref[...] = proj(C_KD, DIFF_WIDTH).astype(jnp.bfloat16)
    vd_ref[...] = proj(C_VD, DIFF_WIDTH).astype(jnp.bfloat16)
    step = 512
    for c0 in range(0, GATE_WIDTH, step):
        gate = proj(C_GATE + c0, step)
        sg_ref[:, c0:c0 + step] = (gate * jax.nn.sigmoid(gate)).astype(sg_ref.dtype)


def _in_proj(h, norm_w, w_in, pq, q_norm, kv_norm, t1, plan):
    rows = h.shape[0]
    tm = plan.tile
    nt = plan.s_pad // tm
    row = lambda width: pl.BlockSpec((tm, width), lambda i: (i, 0))
    const = lambda shape: pl.BlockSpec(shape, lambda i: (0,) * len(shape),
                                       pipeline_mode=pl.Buffered(1))
    bf = jnp.bfloat16
    out_shape = (
        jax.ShapeDtypeStruct((plan.batch, 2, plan.s_pad, F_WIDTH), bf),
        jax.ShapeDtypeStruct((rows, Q_LORA), bf),
        jax.ShapeDtypeStruct((rows, KV_LORA), bf),
        jax.ShapeDtypeStruct((rows, 2 * QK_ROPE), bf),
        jax.ShapeDtypeStruct((rows, DIFF_WIDTH), bf),
        jax.ShapeDtypeStruct((rows, DIFF_WIDTH), bf),
        jax.ShapeDtypeStruct((rows, DIFF_WIDTH), bf),
        jax.ShapeDtypeStruct((rows, GATE_WIDTH), bf),
    )
    out_specs = (
        pl.BlockSpec((1, 2, tm, F_WIDTH), lambda i: (i // nt, 0, i % nt, 0)),
        row(Q_LORA), row(KV_LORA), row(2 * QK_ROPE),
        row(DIFF_WIDTH), row(DIFF_WIDTH), row(DIFF_WIDTH), row(GATE_WIDTH),
    )
    in_specs = [
        row(D_MODEL),
        const((1, D_MODEL)),
        const((D_MODEL, IN_COLS)),
        const((F_GROUPS, F_GROUP_DIM, 2 * F_GROUP_DIM)),
        const((1, Q_LORA)),
        const((1, KV_LORA)),
        pl.BlockSpec((tm, 2 * QK_ROPE), lambda i: (i % nt, 0)),
    ]
    return pl.pallas_call(
        _in_proj_kernel,
        out_shape=out_shape,
        grid=(rows // tm,),
        in_specs=in_specs,
        out_specs=out_specs,
        compiler_params=pltpu.CompilerParams(
            dimension_semantics=("arbitrary",), vmem_limit_bytes=_vmem_limit(56 << 20)),
        name="in_proj",
    )(h, norm_w, w_in, pq, q_norm, kv_norm, t1)


def _mla_up_kernel(cq_ref, ckv_ref, kr_ref, t1_ref, wq_ref, wkv_ref, q_ref, k_ref, v_ref):
    cq = cq_ref[...]
    ckv = ckv_ref[...]
    t1 = t1_ref[...]
    k_rope = kr_ref[:, :QK_ROPE]
    for h in range(MLA_HEADS):
        r = jnp.dot(cq, wq_ref[h], preferred_element_type=jnp.float32) * MLA_QSCALE
        q_ref[h, :, :QK_NOPE] = r[:, :QK_NOPE].astype(jnp.bfloat16)
        rope = _rope_fold(r[:, QK_NOPE:], t1)
        q_ref[h, :, QK_NOPE:] = rope[:, :QK_ROPE].astype(jnp.bfloat16)
        kv = jnp.dot(ckv, wkv_ref[h], preferred_element_type=jnp.float32)
        k_ref[h, :, :QK_NOPE] = kv[:, :QK_NOPE].astype(jnp.bfloat16)
        k_ref[h, :, QK_NOPE:] = k_rope
        v_ref[h] = kv[:, QK_NOPE:].astype(jnp.bfloat16)


def _mla_up(cq, ckv, kr, t1, wq, wkv, plan):
    rows = cq.shape[0]
    tm = plan.tile
    nt = plan.s_pad // tm
    dqk = QK_NOPE + QK_ROPE
    bf = jnp.bfloat16
    row = lambda width: pl.BlockSpec((tm, width), lambda i: (i, 0))
    heads = lambda width: pl.BlockSpec((MLA_HEADS, tm, width), lambda i: (0, i, 0))
    const = lambda shape: pl.BlockSpec(shape, lambda i: (0,) * len(shape))
    return pl.pallas_call(
        _mla_up_kernel,
        out_shape=(jax.ShapeDtypeStruct((MLA_HEADS, rows, dqk), bf),
                   jax.ShapeDtypeStruct((MLA_HEADS, rows, dqk), bf),
                   jax.ShapeDtypeStruct((MLA_HEADS, rows, V_HEAD), bf)),
        grid=(rows // tm,),
        in_specs=[row(Q_LORA), row(KV_LORA), row(2 * QK_ROPE),
                  pl.BlockSpec((tm, 2 * QK_ROPE), lambda i: (i % nt, 0)),
                  const((MLA_HEADS, Q_LORA, QK_NOPE + 2 * QK_ROPE)),
                  const((MLA_HEADS, KV_LORA, QK_NOPE + V_HEAD))],
        out_specs=(heads(dqk), heads(dqk), heads(V_HEAD)),
        compiler_params=pltpu.CompilerParams(dimension_semantics=("arbitrary",)),
        name="mla_up",
    )(cq, ckv, kr, t1, wq, wkv)


def _dft_inner_kernel(g_ref, x_ref, o_ref):
    a = x_ref[0, 0]
    b = x_ref[0, 1]
    dot = functools.partial(jnp.dot, preferred_element_type=jnp.float32)
    o_ref[0, 0] = dot(g_ref[0, 0], a) + dot(g_ref[0, 1], b)
    o_ref[0, 1] = dot(g_ref[1, 0], a) + dot(g_ref[1, 1], b)


def _dft_inner(g, x):
    batch, _, m2, width = x.shape
    tn = DFT_COL_TILE
    return pl.pallas_call(
        _dft_inner_kernel,
        out_shape=jax.ShapeDtypeStruct((batch, 2, m2, width), jnp.float32),
        grid=(batch, width // tn),
        in_specs=[pl.BlockSpec((2, 2, m2, m2), lambda b, j: (0, 0, 0, 0)),
                  pl.BlockSpec((1, 2, m2, tn), lambda b, j: (b, 0, 0, j))],
        out_specs=pl.BlockSpec((1, 2, m2, tn), lambda b, j: (b, 0, 0, j)),
        compiler_params=pltpu.CompilerParams(
            dimension_semantics=("arbitrary", "arbitrary"),
            vmem_limit_bytes=_vmem_limit(40 << 20)),
        name="dft_inner",
    )(g, x)


def _dft_outer_kernel(tw_ref, y_ref, o_ref):
    def body(k1, carry):
        tw = tw_ref[k1]
        acc = jnp.zeros(o_ref.shape[2:], jnp.float32)
        for n1 in range(DFT_RADIX):
            cols = slice(n1 * F_WIDTH, (n1 + 1) * F_WIDTH)
            acc = acc + tw[:, n1:n1 + 1] * y_ref[0, 0, :, cols]
            acc = acc + tw[:, DFT_RADIX + n1:DFT_RADIX + n1 + 1] * y_ref[0, 1, :, cols]
        o_ref[0, k1] = acc
        return carry

    lax.fori_loop(0, DFT_RADIX, body, 0)


def _dft_outer(tw, y):
    batch, _, m2, width = y.shape
    t2 = max(d for d in range(8, m2 + 1, 8) if m2 % d == 0 and d <= DFT_ROW_TILE)
    return pl.pallas_call(
        _dft_outer_kernel,
        out_shape=jax.ShapeDtypeStruct((batch, DFT_RADIX, m2, F_WIDTH), jnp.float32),
        grid=(batch, m2 // t2),
        in_specs=[pl.BlockSpec((DFT_RADIX, t2, 2 * DFT_RADIX), lambda b, j: (0, j, 0)),
                  pl.BlockSpec((1, 2, t2, width), lambda b, j: (b, 0, j, 0))],
        out_specs=pl.BlockSpec((1, DFT_RADIX, t2, F_WIDTH), lambda b, j: (b, 0, j, 0)),
        compiler_params=pltpu.CompilerParams(
            dimension_semantics=("arbitrary", "arbitrary"),
            vmem_limit_bytes=_vmem_limit(40 << 20)),
        name="dft_outer",
    )(tw, y)


def _seq_dft(tables, ab, plan):
    g, tw = tables
    batch, s_pad, n2 = plan.batch, plan.s_pad, plan.seq_len // DFT_RADIX
    m2 = s_pad // DFT_RADIX
    y = _dft_inner(g, ab.reshape(batch, 2, m2, DFT_RADIX * F_WIDTH))
    out = _dft_outer(tw, y)[:, :, :n2].reshape(batch, DFT_RADIX * n2, F_WIDTH)
    out = jnp.pad(out, ((0, 0), (0, s_pad - DFT_RADIX * n2), (0, 0)))
    return out.reshape(batch * s_pad, F_WIDTH)


def _key_chunks(plan):
    k_end = _round_up(plan.seq_len, LANES)
    return [(c0, min(plan.chunk, k_end - c0)) for c0 in range(0, k_end, plan.chunk)]


def _mask_pad_keys(s, c0, seq_len):
    width = s.shape[-1]
    n_valid = seq_len - c0
    if n_valid >= width:
        return s
    a = width - LANES
    assert n_valid > a
    lane = lax.broadcasted_iota(jnp.int32, (s.shape[0], LANES), 1)
    tail = jnp.where(lane < n_valid - a, s[:, a:], MASK_VALUE)
    return jnp.concatenate([s[:, :a], tail], axis=1) if a else tail


def _softmax_step(s, vx, state):
    if state is None:
        m_new = jnp.max(s, axis=-1, keepdims=True)
        p = jnp.exp2((s - m_new).astype(jnp.bfloat16))
        return m_new, jnp.dot(p, vx, preferred_element_type=jnp.float32)
    m, acc = state
    m_new = jnp.maximum(m, jnp.max(s, axis=-1, keepdims=True))
    p = jnp.exp2((s - m_new).astype(jnp.bfloat16))
    alpha = jnp.exp2(m - m_new)
    return m_new, alpha * acc + jnp.dot(p, vx, preferred_element_type=jnp.float32)


def _normalized(acc):
    return acc[:, :V_HEAD] / acc[:, V_HEAD:V_HEAD + 1]


def _fill_value_ext(vx_ref, v):
    vx_ref[:, :V_HEAD] = v
    lane = lax.broadcasted_iota(jnp.int32, v.shape, 1)
    vx_ref[:, V_HEAD:] = jnp.where(lane == 0, 1.0, 0.0).astype(vx_ref.dtype)


def _qk(q, k):
    return lax.dot_general(q, k, (((1,), (1,)), ((), ())), preferred_element_type=jnp.float32)


def _tile_rows(tile, tq):
    return pl.ds(pl.multiple_of(tile * tq, tq), tq)


def _attn_pipeline(n_tiles, head, tail):
    def body(i, carry):
        tail(jnp.maximum(i - 1, 0))
        head(i)
        return carry

    lax.fori_loop(0, n_tiles, body, 0)
    tail(n_tiles - 1)


def _init_attn_state(s_last, m_sc, acc_sc):
    s_last[...] = jnp.zeros(s_last.shape, jnp.float32)
    m_sc[...] = jnp.zeros(m_sc.shape, jnp.float32)
    acc_sc[...] = jnp.ones(acc_sc.shape, jnp.float32)


def _store_attn_state(state, m_sc, acc_sc):
    if state is None:
        m_sc[...] = jnp.full(m_sc.shape, MASK_VALUE, jnp.float32)
        acc_sc[...] = jnp.zeros(acc_sc.shape, jnp.float32)
    else:
        m_sc[...], acc_sc[...] = state


def _mla_attn_kernel(q_ref, k_ref, v_ref, o_ref, vx_ref, s_last, m_sc, acc_sc,
                     *, chunks, seq_len, tq):
    c_last, w_last = chunks[-1]
    _init_attn_state(s_last, m_sc, acc_sc)
    _fill_value_ext(vx_ref, v_ref[0])

    def head(tile):
        q = q_ref[0, _tile_rows(tile, tq), :]
        state = None
        for c0, width in chunks[:-1]:
            s = _qk(q, k_ref[0, c0:c0 + width, :])
            state = _softmax_step(s, vx_ref[c0:c0 + width, :], state)
        s = _qk(q, k_ref[0, c_last:c_last + w_last, :])
        s_last[...] = _mask_pad_keys(s, c_last, seq_len)
        _store_attn_state(state, m_sc, acc_sc)

    def tail(tile):
        _, acc = _softmax_step(s_last[...], vx_ref[c_last:c_last + w_last, :],
                               (m_sc[...], acc_sc[...]))
        o_ref[_tile_rows(tile, tq), :] = _normalized(acc).astype(o_ref.dtype)

    _attn_pipeline(q_ref.shape[1] // tq, head, tail)


def _mla_attn(q, k, v, plan):
    tq = plan.tile
    dqk = QK_NOPE + QK_ROPE
    chunks = _key_chunks(plan)
    per_head = lambda width: pl.BlockSpec((1, plan.s_pad, width), lambda b, h: (h, b, 0))
    return pl.pallas_call(
        functools.partial(_mla_attn_kernel, chunks=chunks, seq_len=plan.seq_len, tq=tq),
        out_shape=jax.ShapeDtypeStruct((plan.batch * plan.s_pad, MLA_WIDTH), jnp.bfloat16),
        grid=(plan.batch, MLA_HEADS),
        in_specs=[per_head(dqk), per_head(dqk), per_head(V_HEAD)],
        out_specs=pl.BlockSpec((plan.s_pad, V_HEAD), lambda b, h: (b, h)),
        scratch_shapes=[pltpu.VMEM((plan.s_pad, 2 * V_HEAD), jnp.bfloat16),
                        pltpu.VMEM((tq, chunks[-1][1]), jnp.float32),
                        pltpu.VMEM((tq, 1), jnp.float32),
                        pltpu.VMEM((tq, 2 * V_HEAD), jnp.float32)],
        compiler_params=pltpu.CompilerParams(
            dimension_semantics=("arbitrary", "arbitrary"),
            vmem_limit_bytes=_vmem_limit(56 << 20)),
        name="mla_attn",
    )(q, k, v)


def _diff_attn_kernel(q_ref, k_ref, v_ref, bt_ref, lam_ref, g_ref, o_ref,
                      vx_ref, s_last, m_sc, acc_sc, *, chunks, seq_len, tq, lam_init):
    q_tiles = tq // LANES
    c_last, w_last = chunks[-1]
    _init_attn_state(s_last, m_sc, acc_sc)
    _fill_value_ext(vx_ref, v_ref[...])
    lv = lam_ref[...]
    lam = (jnp.exp(jnp.sum(lv[0:1] * lv[1:2], axis=-1, keepdims=True))
           - jnp.exp(jnp.sum(lv[2:3] * lv[3:4], axis=-1, keepdims=True)) + lam_init)

    def scores(q2x, tile, c0, width):
        s = _qk(q2x, k_ref[c0:c0 + width, :])
        parts = []
        for g in range(width // LANES):
            o = (c0 // LANES + g) - tile * q_tiles
            bias = bt_ref[0, jnp.clip(o, -BIAS_SIDE, q_tiles + BIAS_SIDE - 1) + BIAS_SIDE]
            sub = s[:, g * LANES:(g + 1) * LANES]
            parts.append(jnp.concatenate([sub[:tq] + bias, sub[tq:] + bias], axis=0))
        return jnp.concatenate(parts, axis=1)

    def head(tile):
        qq = q_ref[_tile_rows(tile, tq), :].astype(jnp.float32)
        lane = lax.broadcasted_iota(jnp.int32, qq.shape, 1)
        zero = jnp.zeros_like(qq)
        q2x = jnp.concatenate([jnp.where(lane < DIFF_QK, qq, zero),
                               jnp.where(lane >= DIFF_QK, qq, zero)], axis=0).astype(jnp.bfloat16)
        state = None
        for c0, width in chunks[:-1]:
            state = _softmax_step(scores(q2x, tile, c0, width), vx_ref[c0:c0 + width, :], state)
        s_last[...] = _mask_pad_keys(scores(q2x, tile, c_last, w_last), c_last, seq_len)
        _store_attn_state(state, m_sc, acc_sc)

    def tail(tile):
        _, acc = _softmax_step(s_last[...], vx_ref[c_last:c_last + w_last, :],
                               (m_sc[...], acc_sc[...]))
        o = _normalized(acc)
        od = o[:tq] - lam * o[tq:]
        o_ref[_tile_rows(tile, tq), :] = (
            _rms(od, g_ref[...], DIFF_NORM_EPS) * (1.0 - lam_init)).astype(o_ref.dtype)

    _attn_pipeline(q_ref.shape[0] // tq, head, tail)


def _diff_attn(qd, kd, vd, bias_tiles, lam_vec, diff_norm, plan, lam_init):
    tq = plan.tile
    nd = bias_tiles.shape[1]
    chunks = _key_chunks(plan)
    per_head = pl.BlockSpec((plan.s_pad, DIFF_V), lambda b, h: (b, h))
    return pl.pallas_call(
        functools.partial(_diff_attn_kernel, chunks=chunks, seq_len=plan.seq_len, tq=tq,
                          lam_init=lam_init),
        out_shape=jax.ShapeDtypeStruct((plan.batch * plan.s_pad, DIFF_WIDTH), jnp.bfloat16),
        grid=(plan.batch, DIFF_HEADS),
        in_specs=[per_head, per_head, per_head,
                  pl.BlockSpec((1, nd, tq, LANES), lambda b, h: (h, 0, 0, 0)),
                  pl.BlockSpec((4, DIFF_QK), lambda b, h: (0, 0)),
                  pl.BlockSpec((1, DIFF_V), lambda b, h: (0, 0))],
        out_specs=per_head,
        scratch_shapes=[pltpu.VMEM((plan.s_pad, 2 * DIFF_V), jnp.bfloat16),
                        pltpu.VMEM((2 * tq, chunks[-1][1]), jnp.float32),
                        pltpu.VMEM((2 * tq, 1), jnp.float32),
                        pltpu.VMEM((2 * tq, 2 * DIFF_V), jnp.float32)],
        compiler_params=pltpu.CompilerParams(
            dimension_semantics=("arbitrary", "arbitrary"),
            vmem_limit_bytes=_vmem_limit(56 << 20)),
        name="diff_attn",
    )(qd, kd, vd, bias_tiles, lam_vec, diff_norm)


def _out_proj_kernel(x_ref, yf_ref, ym_ref, yd_ref, sg_ref, wo_ref, fn_ref, o_ref, *, final):
    c1 = F_WIDTH
    c2 = F_WIDTH + MLA_WIDTH

    def seg(y, c0, c1_):
        yg = (y.astype(jnp.float32) * sg_ref[:, c0:c1_].astype(jnp.float32)).astype(jnp.bfloat16)
        return jnp.dot(yg, wo_ref[c0:c1_, :], preferred_element_type=jnp.float32)

    o = (x_ref[...] + seg(yf_ref[...], 0, c1) + seg(ym_ref[...], c1, c2)
         + seg(yd_ref[...], c2, D_MODEL))
    if final:
        o = _rms(o, fn_ref[...], NORM_EPS)
    o_ref[...] = o


def _out_proj(h, yf, ym, yd, sg, wo, final_norm, plan, final):
    rows = h.shape[0]
    tm = plan.tile
    row = lambda width: pl.BlockSpec((tm, width), lambda i: (i, 0))
    const = lambda shape: pl.BlockSpec(shape, lambda *_: (0,) * len(shape),
                                       pipeline_mode=pl.Buffered(1))
    params = pltpu.CompilerParams(vmem_limit_bytes=_vmem_limit(48 << 20))
    args = (h, yf, ym, yd, sg, wo, final_norm)
    widths = (D_MODEL, F_WIDTH, MLA_WIDTH, DIFF_WIDTH, GATE_WIDTH)
    consts = [const((D_MODEL, D_MODEL)), const((1, D_MODEL))]
    if not final:
        return pl.pallas_call(
            functools.partial(_out_proj_kernel, final=False),
            out_shape=jax.ShapeDtypeStruct((rows, D_MODEL), jnp.float32),
            grid=(rows // tm,),
            in_specs=[row(w) for w in widths] + consts,
            out_specs=row(D_MODEL),
            compiler_params=params,
            name="out_proj",
        )(*args)
    s_real = plan.seq_len - N_META
    tf = _largest_tile(s_real, FINAL_ROW_TILE)
    shifted = lambda width: pl.BlockSpec(
        (pl.Element(tf), pl.Element(width)),
        lambda b, j: (pl.multiple_of(b * plan.s_pad + N_META + j * tf, N_META), 0))
    return pl.pallas_call(
        functools.partial(_out_proj_kernel, final=True),
        out_shape=jax.ShapeDtypeStruct((plan.batch, s_real, D_MODEL), jnp.float32),
        grid=(plan.batch, s_real // tf),
        in_specs=[shifted(w) for w in widths] + consts,
        out_specs=pl.BlockSpec((None, tf, D_MODEL), lambda b, j: (b, j, 0)),
        compiler_params=params,
        name="out_proj_final",
    )(*args)


def _t5_bucket(rel):
    nb = REL_BUCKETS // 2
    max_exact = nb // 2
    ret = (rel > 0).astype(jnp.int32) * nb
    n = jnp.abs(rel)
    nf = jnp.maximum(n, 1).astype(jnp.float32)
    large = max_exact + (jnp.log(nf / max_exact) / math.log(REL_MAX_DIST / max_exact)
                         * (nb - max_exact)).astype(jnp.int32)
    large = jnp.minimum(large, nb - 1)
    return ret + jnp.where(n < max_exact, n, large)


def _bucket_tiles(tq):
    assert LANES >= REL_MAX_DIST
    o = jnp.arange(-BIAS_SIDE, tq // LANES + BIAS_SIDE, dtype=jnp.int32)[:, None, None] * LANES
    row = jnp.arange(tq, dtype=jnp.int32)[None, :, None]
    col = jnp.arange(LANES, dtype=jnp.int32)[None, None, :]
    return _t5_bucket(o + col - row)


def _rope_table(s_pad):
    pos = jnp.arange(s_pad, dtype=jnp.float32)
    inv_freq = ROPE_THETA ** (-jnp.arange(0, QK_ROPE, 2, dtype=jnp.float32) / QK_ROPE)
    ang = pos[:, None] * inv_freq[None, :]
    cos, sin = jnp.cos(ang), jnp.sin(ang)
    return jnp.concatenate([cos, cos, -sin, sin], axis=-1)


def _dft_angle_tables(n, rows, cols):
    r = (jnp.arange(rows, dtype=jnp.int32)[:, None] * jnp.arange(cols, dtype=jnp.int32)[None, :]) % n
    ang = r.astype(jnp.float32) * (2.0 * math.pi / n)
    return jnp.cos(ang), jnp.sin(ang)


def _seq_dft_tables(plan):
    seq_len = plan.seq_len
    assert seq_len % DFT_RADIX == 0 and plan.s_pad % DFT_RADIX == 0
    n2 = seq_len // DFT_RADIX
    m2 = plan.s_pad // DFT_RADIX
    c, s = _dft_angle_tables(n2, m2, m2)
    idx = jnp.arange(m2, dtype=jnp.int32)
    valid = (idx[:, None] < n2) & (idx[None, :] < n2)
    zero = jnp.zeros((), jnp.float32)
    c = jnp.where(valid, c, zero)
    s = jnp.where(valid, s, zero)
    g = jnp.stack([jnp.stack([c, s]), jnp.stack([-s, c])]).astype(jnp.bfloat16)
    k = (jnp.arange(m2, dtype=jnp.int32)[None, :, None]
         + n2 * jnp.arange(DFT_RADIX, dtype=jnp.int32)[:, None, None])
    r = (k * jnp.arange(DFT_RADIX, dtype=jnp.int32)[None, None, :]) % seq_len
    ang = r.astype(jnp.float32) * (2.0 * math.pi / seq_len)
    tw = jnp.concatenate([jnp.cos(ang), jnp.sin(ang)], axis=-1) * (1.0 / math.sqrt(seq_len))
    return g, tw


def _layer_weights(l, w_in, w_uq, w_ukv, w_o):
    bf = jnp.bfloat16
    w = w_in[l]
    s = np.cumsum([0, F_WIDTH, Q_LORA, KV_LORA, QK_ROPE, DIFF_WIDTH, DIFF_WIDTH, DIFF_WIDTH,
                   GATE_WIDTH])
    uf, cq, ckv, kr, qd, kd, vd, gate = (w[:, s[i]:s[i + 1]] for i in range(8))
    half = QK_ROPE // 2
    kr_sw = jnp.concatenate([kr[:, half:], kr[:, :half]], axis=1)
    w_in_r = jnp.concatenate([uf, cq, ckv, kr, kr_sw, qd, kd, vd, gate], axis=1).astype(bf)
    wq = w_uq[l].reshape(Q_LORA, MLA_HEADS, QK_NOPE + QK_ROPE)
    rope = wq[..., QK_NOPE:]
    rope_sw = jnp.concatenate([rope[..., half:], rope[..., :half]], axis=-1)
    wq = jnp.concatenate([wq, rope_sw], axis=-1).transpose(1, 0, 2).astype(bf)
    wkv = w_ukv[l].reshape(KV_LORA, MLA_HEADS, QK_NOPE + V_HEAD).transpose(1, 0, 2).astype(bf)
    return w_in_r, wq, wkv, w_o[l].astype(bf)


def _encode(x, plan, meta_tokens, bias_tiles, final_norm, norm_w, q_norm, kv_norm, diff_norm,
            lam_vecs, pq, layer_w):
    batch, seq_len, s_pad = plan.batch, plan.seq_len, plan.s_pad
    rows = batch * s_pad
    meta = jnp.broadcast_to(meta_tokens[None], (batch, N_META, D_MODEL))
    pad = jnp.zeros((batch, s_pad - seq_len, D_MODEL), x.dtype)
    h = jnp.concatenate([meta, x, pad], axis=1).reshape(rows, D_MODEL)

    t1 = _rope_table(s_pad)
    dft_tables = _seq_dft_tables(plan)
    for l in range(DEPTH):
        w_in_r, wq, wkv, wo = layer_w[l]
        lam_init = 0.8 - 0.6 * math.exp(-0.3 * l)
        ab, cq, ckv, kr, qd, kd, vd, sg = _in_proj(
            h, norm_w[l][None], w_in_r, pq[l], q_norm[l][None], kv_norm[l][None], t1, plan)
        q, k, v = _mla_up(cq, ckv, kr, t1, wq, wkv, plan)
        yf = _seq_dft(dft_tables, ab, plan)
        ym = _mla_attn(q, k, v, plan)
        yd = _diff_attn(qd, kd, vd, bias_tiles, lam_vecs[l], diff_norm[l][None], plan, lam_init)
        h = _out_proj(h, yf, ym, yd, sg, wo, final_norm[None], plan, final=(l == DEPTH - 1))
    return h


def kernel(x_prompt, x_sample, meta_tokens, rel_bias, final_norm, norm_w, w_in, w_fmix, q_norm,
           w_uq, kv_norm, w_ukv, lam_q1, lam_k1, lam_q2, lam_k2, diff_norm, w_o):
    cc, sc = _dft_angle_tables(F_GROUP_DIM, F_GROUP_DIM, F_GROUP_DIM)
    chan_cs = jnp.stack([cc, -sc]) * (1.0 / math.sqrt(F_GROUP_DIM))
    pq = _fold_fmix(chan_cs, w_fmix).astype(jnp.bfloat16)
    lam_vecs = jnp.stack([lam_q1, lam_k1, lam_q2, lam_k2], axis=1)
    layer_w = [_layer_weights(l, w_in, w_uq, w_ukv, w_o) for l in range(DEPTH)]
    outs = []
    bias_by_tile = {}
    for x in (x_prompt, x_sample):
        plan = _plan(x.shape[0], x.shape[1])
        if plan.tile not in bias_by_tile:
            bias_by_tile[plan.tile] = _bias_tiles(rel_bias, _bucket_tiles(plan.tile))
        outs.append(_encode(
            x, plan, meta_tokens, bias_by_tile[plan.tile], final_norm, norm_w, q_norm, kv_norm,
            diff_norm, lam_vecs, pq, layer_w))
    return tuple(outs)
```

```python
import functools
import math
from typing import NamedTuple

import jax
import jax.numpy as jnp
import numpy as np
from jax import lax
from jax.experimental import pallas as pl
from jax.experimental.pallas import tpu as pltpu

D_MODEL = 2048
DEPTH = 2
N_META = 16
F_GROUPS = 4
F_GROUP_DIM = 128
F_WIDTH = F_GROUPS * F_GROUP_DIM
MLA_HEADS = 8
Q_LORA = 768
KV_LORA = 512
QK_NOPE = 128
QK_ROPE = 64
V_HEAD = 128
MLA_WIDTH = MLA_HEADS * V_HEAD
ROPE_THETA = 10000.0
DIFF_HEADS = 4
DIFF_QK = 64
DIFF_V = 2 * DIFF_QK
DIFF_WIDTH = DIFF_HEADS * DIFF_V
REL_BUCKETS = 32
REL_MAX_DIST = 128
NORM_EPS = 1e-6
DIFF_NORM_EPS = 1e-5
GATE_WIDTH = D_MODEL

LOG2E = math.log2(math.e)
MLA_QSCALE = LOG2E / math.sqrt(QK_NOPE + QK_ROPE)
DIFF_QSCALE = LOG2E / math.sqrt(DIFF_QK)
MASK_VALUE = -1e30

LANES = 128
VMEM_BYTES_V7X = 64 * 1024 * 1024

C_UF = 0
C_CQ = C_UF + F_WIDTH
C_CKV = C_CQ + Q_LORA
C_KR = C_CKV + KV_LORA
C_QD = C_KR + 2 * QK_ROPE
C_KD = C_QD + DIFF_WIDTH
C_VD = C_KD + DIFF_WIDTH
C_GATE = C_VD + DIFF_WIDTH
IN_COLS = C_GATE + GATE_WIDTH

MAX_ROW_TILE = 384
MAX_KEY_CHUNK = 2816
DFT_RADIX = 16
DFT_COL_TILE = 1024
DFT_ROW_TILE = 88
BIAS_SIDE = 2
FINAL_ROW_TILE = 256


class SeqPlan(NamedTuple):
    batch: int
    seq_len: int
    s_pad: int
    tile: int
    chunk: int


def _round_up(n, m):
    return -(-n // m) * m


def _largest_tile(s_pad, limit):
    n = s_pad // LANES
    return LANES * max(d for d in range(1, n + 1) if n % d == 0 and d * LANES <= limit)


def _plan(batch, s_real):
    seq_len = s_real + N_META
    tile = min(MAX_ROW_TILE, _round_up(seq_len, LANES))
    s_pad = _round_up(seq_len, tile)
    return SeqPlan(batch, seq_len, s_pad, tile, _largest_tile(s_pad, MAX_KEY_CHUNK))


def _vmem_limit(nbytes):
    return int(min(nbytes, VMEM_BYTES_V7X - (8 << 20)))


def _rms(x, g, eps):
    ms = jnp.mean(x * x, axis=-1, keepdims=True)
    return x * lax.rsqrt(ms + eps) * g


def _rope_fold(v, t1):
    w = v * t1
    return w + pltpu.roll(w, QK_ROPE, 1)


def _fold_fmix_kernel(cs_ref, w_ref, o_ref):
    w = w_ref[0, 0]
    o_ref[0, 0, :, :F_GROUP_DIM] = jnp.dot(
        cs_ref[0], w, preferred_element_type=jnp.float32,
        precision=lax.Precision.HIGHEST)
    o_ref[0, 0, :, F_GROUP_DIM:] = jnp.dot(
        cs_ref[1], w, preferred_element_type=jnp.float32,
        precision=lax.Precision.HIGHEST)


def _fold_fmix(chan_cs, w_fmix):
    c = F_GROUP_DIM
    return pl.pallas_call(
        _fold_fmix_kernel,
        out_shape=jax.ShapeDtypeStruct((DEPTH, F_GROUPS, c, 2 * c), jnp.float32),
        grid=(DEPTH, F_GROUPS),
        in_specs=[pl.BlockSpec((2, c, c), lambda l, g: (0, 0, 0)),
                  pl.BlockSpec((1, 1, c, c), lambda l, g: (l, g, 0, 0))],
        out_specs=pl.BlockSpec((1, 1, c, 2 * c), lambda l, g: (l, g, 0, 0)),
        name="fold_fmix",
    )(chan_cs, w_fmix)


def _bias_tiles_kernel(tab_ref, bucket_ref, o_ref):
    h = pl.program_id(1)
    bucket = bucket_ref[0]
    acc = jnp.zeros(bucket.shape, jnp.float32)
    for b in range(REL_BUCKETS):
        acc = jnp.where(bucket == b, tab_ref[b * DIFF_HEADS + h], acc)
    o_ref[0, 0] = acc * LOG2E


def _bias_tiles(rel_bias, bucket_tiles):
    nd, t, w = bucket_tiles.shape
    return pl.pallas_call(
        _bias_tiles_kernel,
        out_shape=jax.ShapeDtypeStruct((DIFF_HEADS, nd, t, w), jnp.float32),
        grid=(nd, DIFF_HEADS),
        in_specs=[pl.BlockSpec(memory_space=pltpu.SMEM),
                  pl.BlockSpec((1, t, w), lambda d, h: (d, 0, 0))],
        out_specs=pl.BlockSpec((1, 1, t, w), lambda d, h: (h, d, 0, 0)),
        name="bias_tiles",
    )(rel_bias.reshape(-1), bucket_tiles)


def _in_proj_kernel(x_ref, nw_ref, w_ref, pq_ref, qn_ref, kvn_ref, t1_ref,
                    ab_ref, cq_ref, ckv_ref, kr_ref, qd_ref, kd_ref, vd_ref, sg_ref):
    x = x_ref[...]
    xn = _rms(x, nw_ref[...], NORM_EPS).astype(jnp.bfloat16)

    def proj(c0, width):
        return jnp.dot(xn, w_ref[:, c0:c0 + width], preferred_element_type=jnp.float32)

    uf = proj(C_UF, F_WIDTH).astype(jnp.bfloat16)
    c = F_GROUP_DIM
    for g in range(F_GROUPS):
        ab = jnp.dot(uf[:, g * c:(g + 1) * c], pq_ref[g], preferred_element_type=jnp.float32)
        ab_ref[0, 0, :, g * c:(g + 1) * c] = ab[:, :c].astype(jnp.bfloat16)
        ab_ref[0, 1, :, g * c:(g + 1) * c] = ab[:, c:].astype(jnp.bfloat16)

    cq_ref[...] = _rms(proj(C_CQ, Q_LORA), qn_ref[...], NORM_EPS).astype(jnp.bfloat16)
    ckv_ref[...] = _rms(proj(C_CKV, KV_LORA), kvn_ref[...], NORM_EPS).astype(jnp.bfloat16)
    kr_ref[...] = _rope_fold(proj(C_KR, 2 * QK_ROPE), t1_ref[...]).astype(jnp.bfloat16)
    qd_ref[...] = (proj(C_QD, DIFF_WIDTH) * DIFF_QSCALE).astype(jnp.bfloat16)
    kd_ref[...] = proj(C_KD, DIFF_WIDTH).astype(jnp.bfloat16)
    vd_ref[...] = proj(C_VD, DIFF_WIDTH).astype(jnp.bfloat16)
    step = 512
    for c0 in range(0, GATE_WIDTH, step):
        gate = proj(C_GATE + c0, step)
        sg_ref[:, c0:c0 + step] = (gate * jax.nn.sigmoid(gate)).astype(sg_ref.dtype)


def _in_proj(h, norm_w, w_in, pq, q_norm, kv_norm, t1, plan):
    rows = h.shape[0]
    tm = plan.tile
    nt = plan.s_pad // tm
    row = lambda width: pl.BlockSpec((tm, width), lambda i: (i, 0))
    const = lambda shape: pl.BlockSpec(shape, lambda i: (0,) * len(shape),
                                       pipeline_mode=pl.Buffered(1))
    bf = jnp.bfloat16
    out_shape = (
        jax.ShapeDtypeStruct((plan.batch, 2, plan.s_pad, F_WIDTH), bf),
        jax.ShapeDtypeStruct((rows, Q_LORA), bf),
        jax.ShapeDtypeStruct((rows, KV_LORA), bf),
        jax.ShapeDtypeStruct((rows, 2 * QK_ROPE), bf),
        jax.ShapeDtypeStruct((rows, DIFF_WIDTH), bf),
        jax.ShapeDtypeStruct((rows, DIFF_WIDTH), bf),
        jax.ShapeDtypeStruct((rows, DIFF_WIDTH), bf),
        jax.ShapeDtypeStruct((rows, GATE_WIDTH), bf),
    )
    out_specs = (
        pl.BlockSpec((1, 2, tm, F_WIDTH), lambda i: (i // nt, 0, i % nt, 0)),
        row(Q_LORA), row(KV_LORA), row(2 * QK_ROPE),
        row(DIFF_WIDTH), row(DIFF_WIDTH), row(DIFF_WIDTH), row(GATE_WIDTH),
    )
    in_specs = [
        row(D_MODEL),
        const((1, D_MODEL)),
        const((D_MODEL, IN_COLS)),
        const((F_GROUPS, F_GROUP_DIM, 2 * F_GROUP_DIM)),
        const((1, Q_LORA)),
        const((1, KV_LORA)),
        pl.BlockSpec((tm, 2 * QK_ROPE), lambda i: (i % nt, 0)),
    ]
    return pl.pallas_call(
        _in_proj_kernel,
        out_shape=out_shape,
        grid=(rows // tm,),
        in_specs=in_specs,
        out_specs=out_specs,
        compiler_params=pltpu.CompilerParams(
            dimension_semantics=("arbitrary",), vmem_limit_bytes=_vmem_limit(56 << 20)),
        name="in_proj",
    )(h, norm_w, w_in, pq, q_norm, kv_norm, t1)


def _mla_up_kernel(cq_ref, ckv_ref, kr_ref, t1_ref, wq_ref, wkv_ref, q_ref, k_ref, v_ref):
    cq = cq_ref[...]
    ckv = ckv_ref[...]
    t1 = t1_ref[...]
    k_rope = kr_ref[:, :QK_ROPE]
    for h in range(MLA_HEADS):
        r = jnp.dot(cq, wq_ref[h], preferred_element_type=jnp.float32) * MLA_QSCALE
        q_ref[h, :, :QK_NOPE] = r[:, :QK_NOPE].astype(jnp.bfloat16)
        rope = _rope_fold(r[:, QK_NOPE:], t1)
        q_ref[h, :, QK_NOPE:] = rope[:, :QK_ROPE].astype(jnp.bfloat16)
        kv = jnp.dot(ckv, wkv_ref[h], preferred_element_type=jnp.float32)
        k_ref[h, :, :QK_NOPE] = kv[:, :QK_NOPE].astype(jnp.bfloat16)
        k_ref[h, :, QK_NOPE:] = k_rope
        v_ref[h] = kv[:, QK_NOPE:].astype(jnp.bfloat16)


def _mla_up(cq, ckv, kr, t1, wq, wkv, plan):
    rows = cq.shape[0]
    tm = plan.tile
    nt = plan.s_pad // tm
    dqk = QK_NOPE + QK_ROPE
    bf = jnp.bfloat16
    row = lambda width: pl.BlockSpec((tm, width), lambda i: (i, 0))
    heads = lambda width: pl.BlockSpec((MLA_HEADS, tm, width), lambda i: (0, i, 0))
    const = lambda shape: pl.BlockSpec(shape, lambda i: (0,) * len(shape))
    return pl.pallas_call(
        _mla_up_kernel,
        out_shape=(jax.ShapeDtypeStruct((MLA_HEADS, rows, dqk), bf),
                   jax.ShapeDtypeStruct((MLA_HEADS, rows, dqk), bf),
                   jax.ShapeDtypeStruct((MLA_HEADS, rows, V_HEAD), bf)),
        grid=(rows // tm,),
        in_specs=[row(Q_LORA), row(KV_LORA), row(2 * QK_ROPE),
                  pl.BlockSpec((tm, 2 * QK_ROPE), lambda i: (i % nt, 0)),
                  const((MLA_HEADS, Q_LORA, QK_NOPE + 2 * QK_ROPE)),
                  const((MLA_HEADS, KV_LORA, QK_NOPE + V_HEAD))],
        out_specs=(heads(dqk), heads(dqk), heads(V_HEAD)),
        compiler_params=pltpu.CompilerParams(dimension_semantics=("arbitrary",)),
        name="mla_up",
    )(cq, ckv, kr, t1, wq, wkv)


def _dft_inner_kernel(g_ref, x_ref, o_ref):
    a = x_ref[0, 0]
    b = x_ref[0, 1]
    dot = functools.partial(jnp.dot, preferred_element_type=jnp.float32)
    o_ref[0, 0] = dot(g_ref[0, 0], a) + dot(g_ref[0, 1], b)
    o_ref[0, 1] = dot(g_ref[1, 0], a) + dot(g_ref[1, 1], b)


def _dft_inner(g, x):
    batch, _, m2, width = x.shape
    tn = DFT_COL_TILE
    return pl.pallas_call(
        _dft_inner_kernel,
        out_shape=jax.ShapeDtypeStruct((batch, 2, m2, width), jnp.float32),
        grid=(batch, width // tn),
        in_specs=[pl.BlockSpec((2, 2, m2, m2), lambda b, j: (0, 0, 0, 0)),
                  pl.BlockSpec((1, 2, m2, tn), lambda b, j: (b, 0, 0, j))],
        out_specs=pl.BlockSpec((1, 2, m2, tn), lambda b, j: (b, 0, 0, j)),
        compiler_params=pltpu.CompilerParams(
            dimension_semantics=("arbitrary", "arbitrary"),
            vmem_limit_bytes=_vmem_limit(40 << 20)),
        name="dft_inner",
    )(g, x)


def _cmul_const(z, w):
    zr, zi = z
    if w == 1:
        return zr, zi
    if w == -1j:
        return zi, -zr
    wr, wi = float(w.real), float(w.imag)
    return wr * zr - wi * zi, wr * zi + wi * zr


def _fft_real_part(z):
    def fft(z):
        n = len(z)
        if n == 1:
            return z
        even, odd = fft(z[0::2]), fft(z[1::2])
        out = [None] * n
        for k in range(n // 2):
            w = 1 if k == 0 else (-1j if 4 * k == n else np.exp(-2j * np.pi * k / n))
            tr, ti = _cmul_const(odd[k], w)
            out[k] = (even[k][0] + tr, even[k][1] + ti)
            out[k + n // 2] = (even[k][0] - tr, even[k][1] - ti)
        return out

    n = len(z)
    even, odd = fft(z[0::2]), fft(z[1::2])
    out = [None] * n
    for k in range(n // 2):
        w = 1 if k == 0 else (-1j if 4 * k == n else np.exp(-2j * np.pi * k / n))
        tr = _cmul_const(odd[k], w)[0]
        out[k] = even[k][0] + tr
        out[k + n // 2] = even[k][0] - tr
    return out


def _dft_outer_kernel(tw_ref, y_ref, o_ref):
    t2 = y_ref.shape[2]
    tw = tw_ref[...]
    cos = [jnp.broadcast_to(tw[:, n:n + 1], (t2, LANES)) for n in range(DFT_RADIX)]
    sin = [jnp.broadcast_to(tw[:, DFT_RADIX + n:DFT_RADIX + n + 1], (t2, LANES))
           for n in range(DFT_RADIX)]
    for slab in range(F_WIDTH // LANES):
        z = []
        for n1 in range(DFT_RADIX):
            cols = slice(n1 * F_WIDTH + slab * LANES, n1 * F_WIDTH + (slab + 1) * LANES)
            yr, yi = y_ref[0, 0, :, cols], y_ref[0, 1, :, cols]
            z.append((cos[n1] * yr + sin[n1] * yi, cos[n1] * yi - sin[n1] * yr))
        for k1, x in enumerate(_fft_real_part(z)):
            o_ref[0, k1, :, slab * LANES:(slab + 1) * LANES] = x


def _dft_outer(tw, y):
    batch, _, m2, width = y.shape
    t2 = max(d for d in range(8, m2 + 1, 8) if m2 % d == 0 and d <= DFT_ROW_TILE)
    return pl.pallas_call(
        _dft_outer_kernel,
        out_shape=jax.ShapeDtypeStruct((batch, DFT_RADIX, m2, F_WIDTH), jnp.float32),
        grid=(batch, m2 // t2),
        in_specs=[pl.BlockSpec((t2, 2 * DFT_RADIX), lambda b, j: (j, 0)),
                  pl.BlockSpec((1, 2, t2, width), lambda b, j: (b, 0, j, 0))],
        out_specs=pl.BlockSpec((1, DFT_RADIX, t2, F_WIDTH), lambda b, j: (b, 0, j, 0)),
        compiler_params=pltpu.CompilerParams(
            dimension_semantics=("arbitrary", "arbitrary"),
            vmem_limit_bytes=_vmem_limit(40 << 20)),
        name="dft_outer",
    )(tw, y)


def _seq_dft(tables, ab, plan):
    g, tw = tables
    batch, s_pad, n2 = plan.batch, plan.s_pad, plan.seq_len // DFT_RADIX
    m2 = s_pad // DFT_RADIX
    y = _dft_inner(g, ab.reshape(batch, 2, m2, DFT_RADIX * F_WIDTH))
    out = _dft_outer(tw, y)[:, :, :n2].reshape(batch, DFT_RADIX * n2, F_WIDTH)
    out = jnp.pad(out, ((0, 0), (0, s_pad - DFT_RADIX * n2), (0, 0)))
    return out.reshape(batch * s_pad, F_WIDTH)


def _key_chunks(plan):
    k_end = _round_up(plan.seq_len, LANES)
    return [(c0, min(plan.chunk, k_end - c0)) for c0 in range(0, k_end, plan.chunk)]


def _mask_pad_keys(s, c0, seq_len):
    width = s.shape[-1]
    n_valid = seq_len - c0
    if n_valid >= width:
        return s
    a = width - LANES
    assert n_valid > a
    lane = lax.broadcasted_iota(jnp.int32, (s.shape[0], LANES), 1)
    tail = jnp.where(lane < n_valid - a, s[:, a:], MASK_VALUE)
    return jnp.concatenate([s[:, :a], tail], axis=1) if a else tail


def _softmax_step(s, vx, state):
    if state is None:
        m_new = jnp.max(s, axis=-1, keepdims=True)
        p = jnp.exp2((s - m_new).astype(jnp.bfloat16))
        return m_new, jnp.dot(p, vx, preferred_element_type=jnp.float32)
    m, acc = state
    m_new = jnp.maximum(m, jnp.max(s, axis=-1, keepdims=True))
    p = jnp.exp2((s - m_new).astype(jnp.bfloat16))
    alpha = jnp.exp2(m - m_new)
    return m_new, alpha * acc + jnp.dot(p, vx, preferred_element_type=jnp.float32)


def _normalized(acc):
    return acc[:, :V_HEAD] / acc[:, V_HEAD:V_HEAD + 1]


def _fill_value_ext(vx_ref, v):
    vx_ref[:, :V_HEAD] = v
    lane = lax.broadcasted_iota(jnp.int32, v.shape, 1)
    vx_ref[:, V_HEAD:] = jnp.where(lane == 0, 1.0, 0.0).astype(vx_ref.dtype)


def _qk(q, k):
    return lax.dot_general(q, k, (((1,), (1,)), ((), ())), preferred_element_type=jnp.float32)


def _tile_rows(tile, tq):
    return pl.ds(pl.multiple_of(tile * tq, tq), tq)


def _attn_pipeline(n_tiles, head, tail):
    def body(i, carry):
        tail(jnp.maximum(i - 1, 0))
        head(i)
        return carry

    lax.fori_loop(0, n_tiles, body, 0)
    tail(n_tiles - 1)


def _init_attn_state(s_last, m_sc, acc_sc):
    s_last[...] = jnp.zeros(s_last.shape, jnp.float32)
    m_sc[...] = jnp.zeros(m_sc.shape, jnp.float32)
    acc_sc[...] = jnp.ones(acc_sc.shape, jnp.float32)


def _store_attn_state(state, m_sc, acc_sc):
    if state is None:
        m_sc[...] = jnp.full(m_sc.shape, MASK_VALUE, jnp.float32)
        acc_sc[...] = jnp.zeros(acc_sc.shape, jnp.float32)
    else:
        m_sc[...], acc_sc[...] = state


def _mla_attn_kernel(q_ref, k_ref, v_ref, o_ref, vx_ref, s_last, m_sc, acc_sc,
                     *, chunks, seq_len, tq):
    c_last, w_last = chunks[-1]
    _init_attn_state(s_last, m_sc, acc_sc)
    _fill_value_ext(vx_ref, v_ref[0])

    def head(tile):
        q = q_ref[0, _tile_rows(tile, tq), :]
        state = None
        for c0, width in chunks[:-1]:
            s = _qk(q, k_ref[0, c0:c0 + width, :])
            state = _softmax_step(s, vx_ref[c0:c0 + width, :], state)
        s = _qk(q, k_ref[0, c_last:c_last + w_last, :])
        s_last[...] = _mask_pad_keys(s, c_last, seq_len)
        _store_attn_state(state, m_sc, acc_sc)

    def tail(tile):
        _, acc = _softmax_step(s_last[...], vx_ref[c_last:c_last + w_last, :],
                               (m_sc[...], acc_sc[...]))
        o_ref[_tile_rows(tile, tq), :] = _normalized(acc).astype(o_ref.dtype)

    _attn_pipeline(q_ref.shape[1] // tq, head, tail)


def _mla_attn(q, k, v, plan):
    tq = plan.tile
    dqk = QK_NOPE + QK_ROPE
    chunks = _key_chunks(plan)
    per_head = lambda width: pl.BlockSpec((1, plan.s_pad, width), lambda b, h: (h, b, 0))
    return pl.pallas_call(
        functools.partial(_mla_attn_kernel, chunks=chunks, seq_len=plan.seq_len, tq=tq),
        out_shape=jax.ShapeDtypeStruct((plan.batch * plan.s_pad, MLA_WIDTH), jnp.bfloat16),
        grid=(plan.batch, MLA_HEADS),
        in_specs=[per_head(dqk), per_head(dqk), per_head(V_HEAD)],
        out_specs=pl.BlockSpec((plan.s_pad, V_HEAD), lambda b, h: (b, h)),
        scratch_shapes=[pltpu.VMEM((plan.s_pad, 2 * V_HEAD), jnp.bfloat16),
                        pltpu.VMEM((tq, chunks[-1][1]), jnp.float32),
                        pltpu.VMEM((tq, 1), jnp.float32),
                        pltpu.VMEM((tq, 2 * V_HEAD), jnp.float32)],
        compiler_params=pltpu.CompilerParams(
            dimension_semantics=("arbitrary", "arbitrary"),
            vmem_limit_bytes=_vmem_limit(56 << 20)),
        name="mla_attn",
    )(q, k, v)


def _diff_attn_kernel(q_ref, k_ref, v_ref, bt_ref, lam_ref, g_ref, o_ref,
                      vx_ref, s_last, m_sc, acc_sc, *, chunks, seq_len, tq, lam_init):
    q_tiles = tq // LANES
    c_last, w_last = chunks[-1]
    _init_attn_state(s_last, m_sc, acc_sc)
    _fill_value_ext(vx_ref, v_ref[...])
    lv = lam_ref[...]
    lam = (jnp.exp(jnp.sum(lv[0:1] * lv[1:2], axis=-1, keepdims=True))
           - jnp.exp(jnp.sum(lv[2:3] * lv[3:4], axis=-1, keepdims=True)) + lam_init)

    def scores(q2x, tile, c0, width):
        s = _qk(q2x, k_ref[c0:c0 + width, :])
        parts = []
        for g in range(width // LANES):
            o = (c0 // LANES + g) - tile * q_tiles
            bias = bt_ref[0, jnp.clip(o, -BIAS_SIDE, q_tiles + BIAS_SIDE - 1) + BIAS_SIDE]
            sub = s[:, g * LANES:(g + 1) * LANES]
            parts.append(jnp.concatenate([sub[:tq] + bias, sub[tq:] + bias], axis=0))
        return jnp.concatenate(parts, axis=1)

    def head(tile):
        qq = q_ref[_tile_rows(tile, tq), :].astype(jnp.float32)
        lane = lax.broadcasted_iota(jnp.int32, qq.shape, 1)
        zero = jnp.zeros_like(qq)
        q2x = jnp.concatenate([jnp.where(lane < DIFF_QK, qq, zero),
                               jnp.where(lane >= DIFF_QK, qq, zero)], axis=0).astype(jnp.bfloat16)
        state = None
        for c0, width in chunks[:-1]:
            state = _softmax_step(scores(q2x, tile, c0, width), vx_ref[c0:c0 + width, :], state)
        s_last[...] = _mask_pad_keys(scores(q2x, tile, c_last, w_last), c_last, seq_len)
        _store_attn_state(state, m_sc, acc_sc)

    def tail(tile):
        _, acc = _softmax_step(s_last[...], vx_ref[c_last:c_last + w_last, :],
                               (m_sc[...], acc_sc[...]))
        o = _normalized(acc)
        od = o[:tq] - lam * o[tq:]
        o_ref[_tile_rows(tile, tq), :] = (
            _rms(od, g_ref[...], DIFF_NORM_EPS) * (1.0 - lam_init)).astype(o_ref.dtype)

    _attn_pipeline(q_ref.shape[0] // tq, head, tail)


def _diff_attn(qd, kd, vd, bias_tiles, lam_vec, diff_norm, plan, lam_init):
    tq = plan.tile
    nd = bias_tiles.shape[1]
    chunks = _key_chunks(plan)
    per_head = pl.BlockSpec((plan.s_pad, DIFF_V), lambda b, h: (b, h))
    return pl.pallas_call(
        functools.partial(_diff_attn_kernel, chunks=chunks, seq_len=plan.seq_len, tq=tq,
                          lam_init=lam_init),
        out_shape=jax.ShapeDtypeStruct((plan.batch * plan.s_pad, DIFF_WIDTH), jnp.bfloat16),
        grid=(plan.batch, DIFF_HEADS),
        in_specs=[per_head, per_head, per_head,
                  pl.BlockSpec((1, nd, tq, LANES), lambda b, h: (h, 0, 0, 0)),
                  pl.BlockSpec((4, DIFF_QK), lambda b, h: (0, 0)),
                  pl.BlockSpec((1, DIFF_V), lambda b, h: (0, 0))],
        out_specs=per_head,
        scratch_shapes=[pltpu.VMEM((plan.s_pad, 2 * DIFF_V), jnp.bfloat16),
                        pltpu.VMEM((2 * tq, chunks[-1][1]), jnp.float32),
                        pltpu.VMEM((2 * tq, 1), jnp.float32),
                        pltpu.VMEM((2 * tq, 2 * DIFF_V), jnp.float32)],
        compiler_params=pltpu.CompilerParams(
            dimension_semantics=("arbitrary", "arbitrary"),
            vmem_limit_bytes=_vmem_limit(56 << 20)),
        name="diff_attn",
    )(qd, kd, vd, bias_tiles, lam_vec, diff_norm)


def _out_proj_kernel(x_ref, yf_ref, ym_ref, yd_ref, sg_ref, wo_ref, fn_ref, o_ref, *, final):
    c1 = F_WIDTH
    c2 = F_WIDTH + MLA_WIDTH

    def seg(y, c0, c1_):
        yg = (y.astype(jnp.float32) * sg_ref[:, c0:c1_].astype(jnp.float32)).astype(jnp.bfloat16)
        return jnp.dot(yg, wo_ref[c0:c1_, :], preferred_element_type=jnp.float32)

    o = (x_ref[...] + seg(yf_ref[...], 0, c1) + seg(ym_ref[...], c1, c2)
         + seg(yd_ref[...], c2, D_MODEL))
    if final:
        o = _rms(o, fn_ref[...], NORM_EPS)
    o_ref[...] = o


def _out_proj(h, yf, ym, yd, sg, wo, final_norm, plan, final):
    rows = h.shape[0]
    tm = plan.tile
    row = lambda width: pl.BlockSpec((tm, width), lambda i: (i, 0))
    const = lambda shape: pl.BlockSpec(shape, lambda *_: (0,) * len(shape),
                                       pipeline_mode=pl.Buffered(1))
    params = pltpu.CompilerParams(vmem_limit_bytes=_vmem_limit(48 << 20))
    args = (h, yf, ym, yd, sg, wo, final_norm)
    widths = (D_MODEL, F_WIDTH, MLA_WIDTH, DIFF_WIDTH, GATE_WIDTH)
    consts = [const((D_MODEL, D_MODEL)), const((1, D_MODEL))]
    if not final:
        return pl.pallas_call(
            functools.partial(_out_proj_kernel, final=False),
            out_shape=jax.ShapeDtypeStruct((rows, D_MODEL), jnp.float32),
            grid=(rows // tm,),
            in_specs=[row(w) for w in widths] + consts,
            out_specs=row(D_MODEL),
            compiler_params=params,
            name="out_proj",
        )(*args)
    s_real = plan.seq_len - N_META
    tf = _largest_tile(s_real, FINAL_ROW_TILE)
    shifted = lambda width: pl.BlockSpec(
        (pl.Element(tf), pl.Element(width)),
        lambda b, j: (pl.multiple_of(b * plan.s_pad + N_META + j * tf, N_META), 0))
    return pl.pallas_call(
        functools.partial(_out_proj_kernel, final=True),
        out_shape=jax.ShapeDtypeStruct((plan.batch, s_real, D_MODEL), jnp.float32),
        grid=(plan.batch, s_real // tf),
        in_specs=[shifted(w) for w in widths] + consts,
        out_specs=pl.BlockSpec((None, tf, D_MODEL), lambda b, j: (b, j, 0)),
        compiler_params=params,
        name="out_proj_final",
    )(*args)


def _t5_bucket(rel):
    nb = REL_BUCKETS // 2
    max_exact = nb // 2
    ret = (rel > 0).astype(jnp.int32) * nb
    n = jnp.abs(rel)
    nf = jnp.maximum(n, 1).astype(jnp.float32)
    large = max_exact + (jnp.log(nf / max_exact) / math.log(REL_MAX_DIST / max_exact)
                         * (nb - max_exact)).astype(jnp.int32)
    large = jnp.minimum(large, nb - 1)
    return ret + jnp.where(n < max_exact, n, large)


def _bucket_tiles(tq):
    assert LANES >= REL_MAX_DIST
    o = jnp.arange(-BIAS_SIDE, tq // LANES + BIAS_SIDE, dtype=jnp.int32)[:, None, None] * LANES
    row = jnp.arange(tq, dtype=jnp.int32)[None, :, None]
    col = jnp.arange(LANES, dtype=jnp.int32)[None, None, :]
    return _t5_bucket(o + col - row)


def _rope_table(s_pad):
    pos = jnp.arange(s_pad, dtype=jnp.float32)
    inv_freq = ROPE_THETA ** (-jnp.arange(0, QK_ROPE, 2, dtype=jnp.float32) / QK_ROPE)
    ang = pos[:, None] * inv_freq[None, :]
    cos, sin = jnp.cos(ang), jnp.sin(ang)
    return jnp.concatenate([cos, cos, -sin, sin], axis=-1)


def _dft_angle_tables(n, rows, cols):
    r = (jnp.arange(rows, dtype=jnp.int32)[:, None] * jnp.arange(cols, dtype=jnp.int32)[None, :]) % n
    ang = r.astype(jnp.float32) * (2.0 * math.pi / n)
    return jnp.cos(ang), jnp.sin(ang)


def _seq_dft_tables(plan):
    seq_len = plan.seq_len
    assert seq_len % DFT_RADIX == 0 and plan.s_pad % DFT_RADIX == 0
    n2 = seq_len // DFT_RADIX
    m2 = plan.s_pad // DFT_RADIX
    c, s = _dft_angle_tables(n2, m2, m2)
    idx = jnp.arange(m2, dtype=jnp.int32)
    valid = (idx[:, None] < n2) & (idx[None, :] < n2)
    zero = jnp.zeros((), jnp.float32)
    c = jnp.where(valid, c, zero)
    s = jnp.where(valid, s, zero)
    g = jnp.stack([jnp.stack([c, s]), jnp.stack([-s, c])]).astype(jnp.bfloat16)
    cw, sw = _dft_angle_tables(seq_len, m2, DFT_RADIX)
    tw = jnp.concatenate([cw, sw], axis=-1) * (1.0 / math.sqrt(seq_len))
    return g, tw


def _layer_weights(l, w_in, w_uq, w_ukv, w_o):
    bf = jnp.bfloat16
    w = w_in[l]
    s = np.cumsum([0, F_WIDTH, Q_LORA, KV_LORA, QK_ROPE, DIFF_WIDTH, DIFF_WIDTH, DIFF_WIDTH,
                   GATE_WIDTH])
    uf, cq, ckv, kr, qd, kd, vd, gate = (w[:, s[i]:s[i + 1]] for i in range(8))
    half = QK_ROPE // 2
    kr_sw = jnp.concatenate([kr[:, half:], kr[:, :half]], axis=1)
    w_in_r = jnp.concatenate([uf, cq, ckv, kr, kr_sw, qd, kd, vd, gate], axis=1).astype(bf)
    wq = w_uq[l].reshape(Q_LORA, MLA_HEADS, QK_NOPE + QK_ROPE)
    rope = wq[..., QK_NOPE:]
    rope_sw = jnp.concatenate([rope[..., half:], rope[..., :half]], axis=-1)
    wq = jnp.concatenate([wq, rope_sw], axis=-1).transpose(1, 0, 2).astype(bf)
    wkv = w_ukv[l].reshape(KV_LORA, MLA_HEADS, QK_NOPE + V_HEAD).transpose(1, 0, 2).astype(bf)
    return w_in_r, wq, wkv, w_o[l].astype(bf)


def _encode(x, plan, meta_tokens, bias_tiles, final_norm, norm_w, q_norm, kv_norm, diff_norm,
            lam_vecs, pq, layer_w):
    batch, seq_len, s_pad = plan.batch, plan.seq_len, plan.s_pad
    rows = batch * s_pad
    meta = jnp.broadcast_to(meta_tokens[None], (batch, N_META, D_MODEL))
    pad = jnp.zeros((batch, s_pad - seq_len, D_MODEL), x.dtype)
    h = jnp.concatenate([meta, x, pad], axis=1).reshape(rows, D_MODEL)

    t1 = _rope_table(s_pad)
    dft_tables = _seq_dft_tables(plan)
    for l in range(DEPTH):
        w_in_r, wq, wkv, wo = layer_w[l]
        lam_init = 0.8 - 0.6 * math.exp(-0.3 * l)
        ab, cq, ckv, kr, qd, kd, vd, sg = _in_proj(
            h, norm_w[l][None], w_in_r, pq[l], q_norm[l][None], kv_norm[l][None], t1, plan)
        q, k, v = _mla_up(cq, ckv, kr, t1, wq, wkv, plan)
        yf = _seq_dft(dft_tables, ab, plan)
        ym = _mla_attn(q, k, v, plan)
        yd = _diff_attn(qd, kd, vd, bias_tiles, lam_vecs[l], diff_norm[l][None], plan, lam_init)
        h = _out_proj(h, yf, ym, yd, sg, wo, final_norm[None], plan, final=(l == DEPTH - 1))
    return h


def kernel(x_prompt, x_sample, meta_tokens, rel_bias, final_norm, norm_w, w_in, w_fmix, q_norm,
           w_uq, kv_norm, w_ukv, lam_q1, lam_k1, lam_q2, lam_k2, diff_norm, w_o):
    cc, sc = _dft_angle_tables(F_GROUP_DIM, F_GROUP_DIM, F_GROUP_DIM)
    chan_cs = jnp.stack([cc, -sc]) * (1.0 / math.sqrt(F_GROUP_DIM))
    pq = _fold_fmix(chan_cs, w_fmix).astype(jnp.bfloat16)
    lam_vecs = jnp.stack([lam_q1, lam_k1, lam_q2, lam_k2], axis=1)
    layer_w = [_layer_weights(l, w_in, w_uq, w_ukv, w_o) for l in range(DEPTH)]
    outs = []
    bias_by_tile = {}
    for x in (x_prompt, x_sample):
        plan = _plan(x.shape[0], x.shape[1])
        if plan.tile not in bias_by_tile:
            bias_by_tile[plan.tile] = _bias_tiles(rel_bias, _bucket_tiles(plan.tile))
        outs.append(_encode(
            x, plan, meta_tokens, bias_by_tile[plan.tile], final_norm, norm_w, q_norm, kv_norm,
            diff_norm, lam_vecs, pq, layer_w))
    return tuple(outs)
```

```python
import functools
import math
from typing import NamedTuple

import jax
import jax.numpy as jnp
import numpy as np
from jax import lax
from jax.experimental import pallas as pl
from jax.experimental.pallas import tpu as pltpu

D_MODEL = 2048
DEPTH = 2
N_META = 16
F_GROUPS = 4
F_GROUP_DIM = 128
F_WIDTH = F_GROUPS * F_GROUP_DIM
MLA_HEADS = 8
Q_LORA = 768
KV_LORA = 512
QK_NOPE = 128
QK_ROPE = 64
V_HEAD = 128
MLA_WIDTH = MLA_HEADS * V_HEAD
ROPE_THETA = 10000.0
DIFF_HEADS = 4
DIFF_QK = 64
DIFF_V = 2 * DIFF_QK
DIFF_WIDTH = DIFF_HEADS * DIFF_V
REL_BUCKETS = 32
REL_MAX_DIST = 128
NORM_EPS = 1e-6
DIFF_NORM_EPS = 1e-5
GATE_WIDTH = D_MODEL

LOG2E = math.log2(math.e)
MLA_QSCALE = LOG2E / math.sqrt(QK_NOPE + QK_ROPE)
DIFF_QSCALE = LOG2E / math.sqrt(DIFF_QK)
MASK_VALUE = -1e30

LANES = 128
VMEM_BYTES_V7X = 64 * 1024 * 1024

C_UF = 0
C_CQ = C_UF + F_WIDTH
C_CKV = C_CQ + Q_LORA
C_KR = C_CKV + KV_LORA
C_QD = C_KR + 2 * QK_ROPE
C_KD = C_QD + DIFF_WIDTH
C_VD = C_KD + DIFF_WIDTH
C_GATE = C_VD + DIFF_WIDTH
IN_COLS = C_GATE + GATE_WIDTH

MAX_ROW_TILE = 384
MAX_KEY_CHUNK = 2816
DFT_RADIX = 16
DFT_COL_TILE = 1024
DIFF_SINGLE_CHUNK = 4224
DFT_ROW_TILE = 88
BIAS_SIDE = 2
FINAL_ROW_TILE = 256


class SeqPlan(NamedTuple):
    batch: int
    seq_len: int
    s_pad: int
    tile: int
    chunk: int


def _round_up(n, m):
    return -(-n // m) * m


def _largest_tile(s_pad, limit):
    n = s_pad // LANES
    return LANES * max(d for d in range(1, n + 1) if n % d == 0 and d * LANES <= limit)


def _plan(batch, s_real):
    seq_len = s_real + N_META
    tile = min(MAX_ROW_TILE, _round_up(seq_len, LANES))
    s_pad = _round_up(seq_len, tile)
    return SeqPlan(batch, seq_len, s_pad, tile, _largest_tile(s_pad, MAX_KEY_CHUNK))


def _vmem_limit(nbytes):
    return int(min(nbytes, VMEM_BYTES_V7X - (8 << 20)))


def _rms(x, g, eps):
    ms = jnp.mean(x * x, axis=-1, keepdims=True)
    return x * lax.rsqrt(ms + eps) * g


def _rope_fold(v, t1):
    w = v * t1
    return w + pltpu.roll(w, QK_ROPE, 1)


def _fold_fmix_kernel(cs_ref, w_ref, o_ref):
    w = w_ref[0, 0]
    o_ref[0, 0, :, :F_GROUP_DIM] = jnp.dot(
        cs_ref[0], w, preferred_element_type=jnp.float32,
        precision=lax.Precision.HIGHEST)
    o_ref[0, 0, :, F_GROUP_DIM:] = jnp.dot(
        cs_ref[1], w, preferred_element_type=jnp.float32,
        precision=lax.Precision.HIGHEST)


def _fold_fmix(chan_cs, w_fmix):
    c = F_GROUP_DIM
    return pl.pallas_call(
        _fold_fmix_kernel,
        out_shape=jax.ShapeDtypeStruct((DEPTH, F_GROUPS, c, 2 * c), jnp.float32),
        grid=(DEPTH, F_GROUPS),
        in_specs=[pl.BlockSpec((2, c, c), lambda l, g: (0, 0, 0)),
                  pl.BlockSpec((1, 1, c, c), lambda l, g: (l, g, 0, 0))],
        out_specs=pl.BlockSpec((1, 1, c, 2 * c), lambda l, g: (l, g, 0, 0)),
        name="fold_fmix",
    )(chan_cs, w_fmix)


def _bias_tiles_kernel(tab_ref, bucket_ref, o_ref):
    h = pl.program_id(1)
    bucket = bucket_ref[0]
    acc = jnp.zeros(bucket.shape, jnp.float32)
    for b in range(REL_BUCKETS):
        acc = jnp.where(bucket == b, tab_ref[b * DIFF_HEADS + h], acc)
    o_ref[0, 0] = acc * LOG2E


def _bias_tiles(rel_bias, bucket_tiles):
    nd, t, w = bucket_tiles.shape
    return pl.pallas_call(
        _bias_tiles_kernel,
        out_shape=jax.ShapeDtypeStruct((DIFF_HEADS, nd, t, w), jnp.float32),
        grid=(nd, DIFF_HEADS),
        in_specs=[pl.BlockSpec(memory_space=pltpu.SMEM),
                  pl.BlockSpec((1, t, w), lambda d, h: (d, 0, 0))],
        out_specs=pl.BlockSpec((1, 1, t, w), lambda d, h: (h, d, 0, 0)),
        name="bias_tiles",
    )(rel_bias.reshape(-1), bucket_tiles)


def _in_proj_kernel(x_ref, nw_ref, w_ref, pq_ref, qn_ref, kvn_ref, t1_ref,
                    ab_ref, cq_ref, ckv_ref, kr_ref, qd_ref, kd_ref, vd_ref, sg_ref):
    x = x_ref[...]
    xn = _rms(x, nw_ref[...], NORM_EPS).astype(jnp.bfloat16)

    def proj(c0, width):
        return jnp.dot(xn, w_ref[:, c0:c0 + width], preferred_element_type=jnp.float32)

    uf = proj(C_UF, F_WIDTH).astype(jnp.bfloat16)
    c = F_GROUP_DIM
    for g in range(F_GROUPS):
        ab = jnp.dot(uf[:, g * c:(g + 1) * c], pq_ref[g], preferred_element_type=jnp.float32)
        ab_ref[0, 0, :, g * c:(g + 1) * c] = ab[:, :c].astype(jnp.bfloat16)
        ab_ref[0, 1, :, g * c:(g + 1) * c] = ab[:, c:].astype(jnp.bfloat16)

    cq_ref[...] = _rms(proj(C_CQ, Q_LORA), qn_ref[...], NORM_EPS).astype(jnp.bfloat16)
    ckv_ref[...] = _rms(proj(C_CKV, KV_LORA), kvn_ref[...], NORM_EPS).astype(jnp.bfloat16)
    kr_ref[...] = _rope_fold(proj(C_KR, 2 * QK_ROPE), t1_ref[...]).astype(jnp.bfloat16)
    qd_ref[...] = (proj(C_QD, DIFF_WIDTH) * DIFF_QSCALE).astype(jnp.bfloat16)
    kd_ref[...] = proj(C_KD, DIFF_WIDTH).astype(jnp.bfloat16)
    vd_ref[...] = proj(C_VD, DIFF_WIDTH).astype(jnp.bfloat16)
    step = 512
    for c0 in range(0, GATE_WIDTH, step):
        gate = proj(C_GATE + c0, step)
        sg_ref[:, c0:c0 + step] = (gate * jax.nn.sigmoid(gate)).astype(sg_ref.dtype)


def _in_proj(h, norm_w, w_in, pq, q_norm, kv_norm, t1, plan):
    rows = h.shape[0]
    tm = plan.tile
    nt = plan.s_pad // tm
    row = lambda width: pl.BlockSpec((tm, width), lambda i: (i, 0))
    const = lambda shape: pl.BlockSpec(shape, lambda i: (0,) * len(shape),
                                       pipeline_mode=pl.Buffered(1))
    bf = jnp.bfloat16
    out_shape = (
        jax.ShapeDtypeStruct((plan.batch, 2, plan.s_pad, F_WIDTH), bf),
        jax.ShapeDtypeStruct((rows, Q_LORA), bf),
        jax.ShapeDtypeStruct((rows, KV_LORA), bf),
        jax.ShapeDtypeStruct((rows, 2 * QK_ROPE), bf),
        jax.ShapeDtypeStruct((rows, DIFF_WIDTH), bf),
        jax.ShapeDtypeStruct((rows, DIFF_WIDTH), bf),
        jax.ShapeDtypeStruct((rows, DIFF_WIDTH), bf),
        jax.ShapeDtypeStruct((rows, GATE_WIDTH), bf),
    )
    out_specs = (
        pl.BlockSpec((1, 2, tm, F_WIDTH), lambda i: (i // nt, 0, i % nt, 0)),
        row(Q_LORA), row(KV_LORA), row(2 * QK_ROPE),
        row(DIFF_WIDTH), row(DIFF_WIDTH), row(DIFF_WIDTH), row(GATE_WIDTH),
    )
    in_specs = [
        row(D_MODEL),
        const((1, D_MODEL)),
        const((D_MODEL, IN_COLS)),
        const((F_GROUPS, F_GROUP_DIM, 2 * F_GROUP_DIM)),
        const((1, Q_LORA)),
        const((1, KV_LORA)),
        pl.BlockSpec((tm, 2 * QK_ROPE), lambda i: (i % nt, 0)),
    ]
    return pl.pallas_call(
        _in_proj_kernel,
        out_shape=out_shape,
        grid=(rows // tm,),
        in_specs=in_specs,
        out_specs=out_specs,
        compiler_params=pltpu.CompilerParams(
            dimension_semantics=("arbitrary",), vmem_limit_bytes=_vmem_limit(56 << 20)),
        name="in_proj",
    )(h, norm_w, w_in, pq, q_norm, kv_norm, t1)


def _mla_up_kernel(cq_ref, ckv_ref, kr_ref, t1_ref, wq_ref, wkv_ref, q_ref, k_ref, v_ref):
    cq = cq_ref[...]
    ckv = ckv_ref[...]
    t1 = t1_ref[...]
    k_rope = kr_ref[:, :QK_ROPE]
    for h in range(MLA_HEADS):
        r = jnp.dot(cq, wq_ref[h], preferred_element_type=jnp.float32) * MLA_QSCALE
        q_ref[h, :, :QK_NOPE] = r[:, :QK_NOPE].astype(jnp.bfloat16)
        rope = _rope_fold(r[:, QK_NOPE:], t1)
        q_ref[h, :, QK_NOPE:] = rope[:, :QK_ROPE].astype(jnp.bfloat16)
        kv = jnp.dot(ckv, wkv_ref[h], preferred_element_type=jnp.float32)
        k_ref[h, :, :QK_NOPE] = kv[:, :QK_NOPE].astype(jnp.bfloat16)
        k_ref[h, :, QK_NOPE:] = k_rope
        v_ref[h] = kv[:, QK_NOPE:].astype(jnp.bfloat16)


def _mla_up(cq, ckv, kr, t1, wq, wkv, plan):
    rows = cq.shape[0]
    tm = plan.tile
    nt = plan.s_pad // tm
    dqk = QK_NOPE + QK_ROPE
    bf = jnp.bfloat16
    row = lambda width: pl.BlockSpec((tm, width), lambda i: (i, 0))
    heads = lambda width: pl.BlockSpec((MLA_HEADS, tm, width), lambda i: (0, i, 0))
    const = lambda shape: pl.BlockSpec(shape, lambda i: (0,) * len(shape))
    return pl.pallas_call(
        _mla_up_kernel,
        out_shape=(jax.ShapeDtypeStruct((MLA_HEADS, rows, dqk), bf),
                   jax.ShapeDtypeStruct((MLA_HEADS, rows, dqk), bf),
                   jax.ShapeDtypeStruct((MLA_HEADS, rows, V_HEAD), bf)),
        grid=(rows // tm,),
        in_specs=[row(Q_LORA), row(KV_LORA), row(2 * QK_ROPE),
                  pl.BlockSpec((tm, 2 * QK_ROPE), lambda i: (i % nt, 0)),
                  const((MLA_HEADS, Q_LORA, QK_NOPE + 2 * QK_ROPE)),
                  const((MLA_HEADS, KV_LORA, QK_NOPE + V_HEAD))],
        out_specs=(heads(dqk), heads(dqk), heads(V_HEAD)),
        compiler_params=pltpu.CompilerParams(dimension_semantics=("arbitrary",)),
        name="mla_up",
    )(cq, ckv, kr, t1, wq, wkv)


def _dft_inner_kernel(g_ref, x_ref, o_ref):
    a = x_ref[0, 0]
    b = x_ref[0, 1]
    dot = functools.partial(jnp.dot, preferred_element_type=jnp.float32)
    o_ref[0, 0] = dot(g_ref[0, 0], a) + dot(g_ref[0, 1], b)
    o_ref[0, 1] = dot(g_ref[1, 0], a) + dot(g_ref[1, 1], b)


def _dft_inner(g, x):
    batch, _, m2, width = x.shape
    tn = DFT_COL_TILE
    return pl.pallas_call(
        _dft_inner_kernel,
        out_shape=jax.ShapeDtypeStruct((batch, 2, m2, width), jnp.float32),
        grid=(batch, width // tn),
        in_specs=[pl.BlockSpec((2, 2, m2, m2), lambda b, j: (0, 0, 0, 0)),
                  pl.BlockSpec((1, 2, m2, tn), lambda b, j: (b, 0, 0, j))],
        out_specs=pl.BlockSpec((1, 2, m2, tn), lambda b, j: (b, 0, 0, j)),
        compiler_params=pltpu.CompilerParams(
            dimension_semantics=("arbitrary", "arbitrary"),
            vmem_limit_bytes=_vmem_limit(40 << 20)),
        name="dft_inner",
    )(g, x)


def _cmul_const(z, w):
    zr, zi = z
    if w == 1:
        return zr, zi
    if w == -1j:
        return zi, -zr
    wr, wi = float(w.real), float(w.imag)
    return wr * zr - wi * zi, wr * zi + wi * zr


def _fft_real_part(z):
    def fft(z):
        n = len(z)
        if n == 1:
            return z
        even, odd = fft(z[0::2]), fft(z[1::2])
        out = [None] * n
        for k in range(n // 2):
            w = 1 if k == 0 else (-1j if 4 * k == n else np.exp(-2j * np.pi * k / n))
            tr, ti = _cmul_const(odd[k], w)
            out[k] = (even[k][0] + tr, even[k][1] + ti)
            out[k + n // 2] = (even[k][0] - tr, even[k][1] - ti)
        return out

    n = len(z)
    even, odd = fft(z[0::2]), fft(z[1::2])
    out = [None] * n
    for k in range(n // 2):
        w = 1 if k == 0 else (-1j if 4 * k == n else np.exp(-2j * np.pi * k / n))
        tr = _cmul_const(odd[k], w)[0]
        out[k] = even[k][0] + tr
        out[k + n // 2] = even[k][0] - tr
    return out


def _dft_outer_kernel(tw_ref, y_ref, o_ref):
    t2 = y_ref.shape[2]
    tw = tw_ref[...]
    cos = [jnp.broadcast_to(tw[:, n:n + 1], (t2, LANES)) for n in range(DFT_RADIX)]
    sin = [jnp.broadcast_to(tw[:, DFT_RADIX + n:DFT_RADIX + n + 1], (t2, LANES))
           for n in range(DFT_RADIX)]
    for slab in range(F_WIDTH // LANES):
        z = []
        for n1 in range(DFT_RADIX):
            cols = slice(n1 * F_WIDTH + slab * LANES, n1 * F_WIDTH + (slab + 1) * LANES)
            yr, yi = y_ref[0, 0, :, cols], y_ref[0, 1, :, cols]
            z.append((cos[n1] * yr + sin[n1] * yi, cos[n1] * yi - sin[n1] * yr))
        for k1, x in enumerate(_fft_real_part(z)):
            o_ref[0, k1, :, slab * LANES:(slab + 1) * LANES] = x


def _dft_outer(tw, y):
    batch, _, m2, width = y.shape
    t2 = max(d for d in range(8, m2 + 1, 8) if m2 % d == 0 and d <= DFT_ROW_TILE)
    return pl.pallas_call(
        _dft_outer_kernel,
        out_shape=jax.ShapeDtypeStruct((batch, DFT_RADIX, m2, F_WIDTH), jnp.float32),
        grid=(batch, m2 // t2),
        in_specs=[pl.BlockSpec((t2, 2 * DFT_RADIX), lambda b, j: (j, 0)),
                  pl.BlockSpec((1, 2, t2, width), lambda b, j: (b, 0, j, 0))],
        out_specs=pl.BlockSpec((1, DFT_RADIX, t2, F_WIDTH), lambda b, j: (b, 0, j, 0)),
        compiler_params=pltpu.CompilerParams(
            dimension_semantics=("arbitrary", "arbitrary"),
            vmem_limit_bytes=_vmem_limit(40 << 20)),
        name="dft_outer",
    )(tw, y)


def _seq_dft(tables, ab, plan):
    g, tw = tables
    batch, s_pad, n2 = plan.batch, plan.s_pad, plan.seq_len // DFT_RADIX
    m2 = s_pad // DFT_RADIX
    y = _dft_inner(g, ab.reshape(batch, 2, m2, DFT_RADIX * F_WIDTH))
    out = _dft_outer(tw, y)[:, :, :n2].reshape(batch, DFT_RADIX * n2, F_WIDTH)
    out = jnp.pad(out, ((0, 0), (0, s_pad - DFT_RADIX * n2), (0, 0)))
    return out.reshape(batch * s_pad, F_WIDTH)


def _key_chunks(plan, single_chunk_limit=0):
    k_end = _round_up(plan.seq_len, LANES)
    chunk = k_end if k_end <= single_chunk_limit else plan.chunk
    return [(c0, min(chunk, k_end - c0)) for c0 in range(0, k_end, chunk)]


def _mask_pad_keys(s, c0, seq_len):
    width = s.shape[-1]
    n_valid = seq_len - c0
    if n_valid >= width:
        return s
    a = width - LANES
    assert n_valid > a
    lane = lax.broadcasted_iota(jnp.int32, (s.shape[0], LANES), 1)
    tail = jnp.where(lane < n_valid - a, s[:, a:], MASK_VALUE)
    return jnp.concatenate([s[:, :a], tail], axis=1) if a else tail


def _softmax_step(s, vx, state):
    if state is None:
        m_new = jnp.max(s, axis=-1, keepdims=True)
        p = jnp.exp2((s - m_new).astype(jnp.bfloat16))
        return m_new, jnp.dot(p, vx, preferred_element_type=jnp.float32)
    m, acc = state
    m_new = jnp.maximum(m, jnp.max(s, axis=-1, keepdims=True))
    p = jnp.exp2((s - m_new).astype(jnp.bfloat16))
    alpha = jnp.exp2(m - m_new)
    return m_new, alpha * acc + jnp.dot(p, vx, preferred_element_type=jnp.float32)


def _normalized(acc):
    return acc[:, :V_HEAD] / acc[:, V_HEAD:V_HEAD + 1]


def _fill_value_ext(vx_ref, v):
    vx_ref[:, :V_HEAD] = v
    lane = lax.broadcasted_iota(jnp.int32, v.shape, 1)
    vx_ref[:, V_HEAD:] = jnp.where(lane == 0, 1.0, 0.0).astype(vx_ref.dtype)


def _qk(q, k):
    return lax.dot_general(q, k, (((1,), (1,)), ((), ())), preferred_element_type=jnp.float32)


def _tile_rows(tile, tq):
    return pl.ds(pl.multiple_of(tile * tq, tq), tq)


def _attn_pipeline(n_tiles, head, tail):
    def body(i, carry):
        tail(jnp.maximum(i - 1, 0))
        head(i)
        return carry

    lax.fori_loop(0, n_tiles, body, 0)
    tail(n_tiles - 1)


def _init_attn_state(s_last, m_sc, acc_sc):
    s_last[...] = jnp.zeros(s_last.shape, jnp.float32)
    m_sc[...] = jnp.zeros(m_sc.shape, jnp.float32)
    acc_sc[...] = jnp.ones(acc_sc.shape, jnp.float32)


def _store_attn_state(state, m_sc, acc_sc):
    if state is None:
        m_sc[...] = jnp.full(m_sc.shape, MASK_VALUE, jnp.float32)
        acc_sc[...] = jnp.zeros(acc_sc.shape, jnp.float32)
    else:
        m_sc[...], acc_sc[...] = state


def _mla_attn_kernel(q_ref, k_ref, v_ref, o_ref, vx_ref, s_last, m_sc, acc_sc,
                     *, chunks, seq_len, tq):
    c_last, w_last = chunks[-1]
    _init_attn_state(s_last, m_sc, acc_sc)
    _fill_value_ext(vx_ref, v_ref[0])

    def head(tile):
        q = q_ref[0, _tile_rows(tile, tq), :]
        state = None
        for c0, width in chunks[:-1]:
            s = _qk(q, k_ref[0, c0:c0 + width, :])
            state = _softmax_step(s, vx_ref[c0:c0 + width, :], state)
        s = _qk(q, k_ref[0, c_last:c_last + w_last, :])
        s_last[...] = _mask_pad_keys(s, c_last, seq_len)
        _store_attn_state(state, m_sc, acc_sc)

    def tail(tile):
        _, acc = _softmax_step(s_last[...], vx_ref[c_last:c_last + w_last, :],
                               (m_sc[...], acc_sc[...]))
        o_ref[_tile_rows(tile, tq), :] = _normalized(acc).astype(o_ref.dtype)

    _attn_pipeline(q_ref.shape[1] // tq, head, tail)


def _mla_attn(q, k, v, plan):
    tq = plan.tile
    dqk = QK_NOPE + QK_ROPE
    chunks = _key_chunks(plan)
    per_head = lambda width: pl.BlockSpec((1, plan.s_pad, width), lambda b, h: (h, b, 0))
    return pl.pallas_call(
        functools.partial(_mla_attn_kernel, chunks=chunks, seq_len=plan.seq_len, tq=tq),
        out_shape=jax.ShapeDtypeStruct((plan.batch * plan.s_pad, MLA_WIDTH), jnp.bfloat16),
        grid=(plan.batch, MLA_HEADS),
        in_specs=[per_head(dqk), per_head(dqk), per_head(V_HEAD)],
        out_specs=pl.BlockSpec((plan.s_pad, V_HEAD), lambda b, h: (b, h)),
        scratch_shapes=[pltpu.VMEM((plan.s_pad, 2 * V_HEAD), jnp.bfloat16),
                        pltpu.VMEM((tq, chunks[-1][1]), jnp.float32),
                        pltpu.VMEM((tq, 1), jnp.float32),
                        pltpu.VMEM((tq, 2 * V_HEAD), jnp.float32)],
        compiler_params=pltpu.CompilerParams(
            dimension_semantics=("arbitrary", "arbitrary"),
            vmem_limit_bytes=_vmem_limit(56 << 20)),
        name="mla_attn",
    )(q, k, v)


def _diff_attn_kernel(q_ref, k_ref, v_ref, bt_ref, lam_ref, g_ref, o_ref,
                      vx_ref, s_last, m_sc, acc_sc, *, chunks, seq_len, tq, lam_init):
    q_tiles = tq // LANES
    c_last, w_last = chunks[-1]
    _init_attn_state(s_last, m_sc, acc_sc)
    _fill_value_ext(vx_ref, v_ref[...])
    lv = lam_ref[...]
    lam = (jnp.exp(jnp.sum(lv[0:1] * lv[1:2], axis=-1, keepdims=True))
           - jnp.exp(jnp.sum(lv[2:3] * lv[3:4], axis=-1, keepdims=True)) + lam_init)

    def scores(q2x, tile, c0, width):
        s = _qk(q2x, k_ref[c0:c0 + width, :])
        parts = []
        for g in range(width // LANES):
            o = (c0 // LANES + g) - tile * q_tiles
            bias = bt_ref[0, jnp.clip(o, -BIAS_SIDE, q_tiles + BIAS_SIDE - 1) + BIAS_SIDE]
            sub = s[:, g * LANES:(g + 1) * LANES]
            parts.append(jnp.concatenate([sub[:tq] + bias, sub[tq:] + bias], axis=0))
        return jnp.concatenate(parts, axis=1)

    def head(tile):
        qq = q_ref[_tile_rows(tile, tq), :].astype(jnp.float32)
        lane = lax.broadcasted_iota(jnp.int32, qq.shape, 1)
        zero = jnp.zeros_like(qq)
        q2x = jnp.concatenate([jnp.where(lane < DIFF_QK, qq, zero),
                               jnp.where(lane >= DIFF_QK, qq, zero)], axis=0).astype(jnp.bfloat16)
        state = None
        for c0, width in chunks[:-1]:
            state = _softmax_step(scores(q2x, tile, c0, width), vx_ref[c0:c0 + width, :], state)
        s_last[...] = _mask_pad_keys(scores(q2x, tile, c_last, w_last), c_last, seq_len)
        _store_attn_state(state, m_sc, acc_sc)

    def tail(tile):
        _, acc = _softmax_step(s_last[...], vx_ref[c_last:c_last + w_last, :],
                               (m_sc[...], acc_sc[...]))
        o = _normalized(acc)
        od = o[:tq] - lam * o[tq:]
        o_ref[_tile_rows(tile, tq), :] = (
            _rms(od, g_ref[...], DIFF_NORM_EPS) * (1.0 - lam_init)).astype(o_ref.dtype)

    _attn_pipeline(q_ref.shape[0] // tq, head, tail)


def _diff_attn(qd, kd, vd, bias_tiles, lam_vec, diff_norm, plan, lam_init):
    tq = plan.tile
    nd = bias_tiles.shape[1]
    chunks = _key_chunks(plan, DIFF_SINGLE_CHUNK)
    per_head = pl.BlockSpec((plan.s_pad, DIFF_V), lambda b, h: (b, h))
    return pl.pallas_call(
        functools.partial(_diff_attn_kernel, chunks=chunks, seq_len=plan.seq_len, tq=tq,
                          lam_init=lam_init),
        out_shape=jax.ShapeDtypeStruct((plan.batch * plan.s_pad, DIFF_WIDTH), jnp.bfloat16),
        grid=(plan.batch, DIFF_HEADS),
        in_specs=[per_head, per_head, per_head,
                  pl.BlockSpec((1, nd, tq, LANES), lambda b, h: (h, 0, 0, 0)),
                  pl.BlockSpec((4, DIFF_QK), lambda b, h: (0, 0)),
                  pl.BlockSpec((1, DIFF_V), lambda b, h: (0, 0))],
        out_specs=per_head,
        scratch_shapes=[pltpu.VMEM((plan.s_pad, 2 * DIFF_V), jnp.bfloat16),
                        pltpu.VMEM((2 * tq, chunks[-1][1]), jnp.float32),
                        pltpu.VMEM((2 * tq, 1), jnp.float32),
                        pltpu.VMEM((2 * tq, 2 * DIFF_V), jnp.float32)],
        compiler_params=pltpu.CompilerParams(
            dimension_semantics=("arbitrary", "arbitrary"),
            vmem_limit_bytes=_vmem_limit(56 << 20)),
        name="diff_attn",
    )(qd, kd, vd, bias_tiles, lam_vec, diff_norm)


def _out_proj_kernel(x_ref, yf_ref, ym_ref, yd_ref, sg_ref, wo_ref, fn_ref, o_ref, *, final):
    c1 = F_WIDTH
    c2 = F_WIDTH + MLA_WIDTH

    def seg(y, c0, c1_):
        yg = (y.astype(jnp.float32) * sg_ref[:, c0:c1_].astype(jnp.float32)).astype(jnp.bfloat16)
        return jnp.dot(yg, wo_ref[c0:c1_, :], preferred_element_type=jnp.float32)

    o = (x_ref[...] + seg(yf_ref[...], 0, c1) + seg(ym_ref[...], c1, c2)
         + seg(yd_ref[...], c2, D_MODEL))
    if final:
        o = _rms(o, fn_ref[...], NORM_EPS)
    o_ref[...] = o


def _out_proj(h, yf, ym, yd, sg, wo, final_norm, plan, final):
    rows = h.shape[0]
    tm = plan.tile
    row = lambda width: pl.BlockSpec((tm, width), lambda i: (i, 0))
    const = lambda shape: pl.BlockSpec(shape, lambda *_: (0,) * len(shape),
                                       pipeline_mode=pl.Buffered(1))
    params = pltpu.CompilerParams(vmem_limit_bytes=_vmem_limit(48 << 20))
    args = (h, yf, ym, yd, sg, wo, final_norm)
    widths = (D_MODEL, F_WIDTH, MLA_WIDTH, DIFF_WIDTH, GATE_WIDTH)
    consts = [const((D_MODEL, D_MODEL)), const((1, D_MODEL))]
    if not final:
        return pl.pallas_call(
            functools.partial(_out_proj_kernel, final=False),
            out_shape=jax.ShapeDtypeStruct((rows, D_MODEL), jnp.float32),
            grid=(rows // tm,),
            in_specs=[row(w) for w in widths] + consts,
            out_specs=row(D_MODEL),
            compiler_params=params,
            name="out_proj",
        )(*args)
    s_real = plan.seq_len - N_META
    tf = _largest_tile(s_real, FINAL_ROW_TILE)
    shifted = lambda width: pl.BlockSpec(
        (pl.Element(tf), pl.Element(width)),
        lambda b, j: (pl.multiple_of(b * plan.s_pad + N_META + j * tf, N_META), 0))
    return pl.pallas_call(
        functools.partial(_out_proj_kernel, final=True),
        out_shape=jax.ShapeDtypeStruct((plan.batch, s_real, D_MODEL), jnp.float32),
        grid=(plan.batch, s_real // tf),
        in_specs=[shifted(w) for w in widths] + consts,
        out_specs=pl.BlockSpec((None, tf, D_MODEL), lambda b, j: (b, j, 0)),
        compiler_params=params,
        name="out_proj_final",
    )(*args)


def _t5_bucket(rel):
    nb = REL_BUCKETS // 2
    max_exact = nb // 2
    ret = (rel > 0).astype(jnp.int32) * nb
    n = jnp.abs(rel)
    nf = jnp.maximum(n, 1).astype(jnp.float32)
    large = max_exact + (jnp.log(nf / max_exact) / math.log(REL_MAX_DIST / max_exact)
                         * (nb - max_exact)).astype(jnp.int32)
    large = jnp.minimum(large, nb - 1)
    return ret + jnp.where(n < max_exact, n, large)


def _bucket_tiles(tq):
    assert LANES >= REL_MAX_DIST
    o = jnp.arange(-BIAS_SIDE, tq // LANES + BIAS_SIDE, dtype=jnp.int32)[:, None, None] * LANES
    row = jnp.arange(tq, dtype=jnp.int32)[None, :, None]
    col = jnp.arange(LANES, dtype=jnp.int32)[None, None, :]
    return _t5_bucket(o + col - row)


def _rope_table(s_pad):
    pos = jnp.arange(s_pad, dtype=jnp.float32)
    inv_freq = ROPE_THETA ** (-jnp.arange(0, QK_ROPE, 2, dtype=jnp.float32) / QK_ROPE)
    ang = pos[:, None] * inv_freq[None, :]
    cos, sin = jnp.cos(ang), jnp.sin(ang)
    return jnp.concatenate([cos, cos, -sin, sin], axis=-1)


def _dft_angle_tables(n, rows, cols):
    r = (jnp.arange(rows, dtype=jnp.int32)[:, None] * jnp.arange(cols, dtype=jnp.int32)[None, :]) % n
    ang = r.astype(jnp.float32) * (2.0 * math.pi / n)
    return jnp.cos(ang), jnp.sin(ang)


def _seq_dft_tables(plan):
    seq_len = plan.seq_len
    assert seq_len % DFT_RADIX == 0 and plan.s_pad % DFT_RADIX == 0
    n2 = seq_len // DFT_RADIX
    m2 = plan.s_pad // DFT_RADIX
    c, s = _dft_angle_tables(n2, m2, m2)
    idx = jnp.arange(m2, dtype=jnp.int32)
    valid = (idx[:, None] < n2) & (idx[None, :] < n2)
    zero = jnp.zeros((), jnp.float32)
    c = jnp.where(valid, c, zero)
    s = jnp.where(valid, s, zero)
    g = jnp.stack([jnp.stack([c, s]), jnp.stack([-s, c])]).astype(jnp.bfloat16)
    cw, sw = _dft_angle_tables(seq_len, m2, DFT_RADIX)
    tw = jnp.concatenate([cw, sw], axis=-1) * (1.0 / math.sqrt(seq_len))
    return g, tw


def _layer_weights(l, w_in, w_uq, w_ukv, w_o):
    bf = jnp.bfloat16
    w = w_in[l]
    s = np.cumsum([0, F_WIDTH, Q_LORA, KV_LORA, QK_ROPE, DIFF_WIDTH, DIFF_WIDTH, DIFF_WIDTH,
                   GATE_WIDTH])
    uf, cq, ckv, kr, qd, kd, vd, gate = (w[:, s[i]:s[i + 1]] for i in range(8))
    half = QK_ROPE // 2
    kr_sw = jnp.concatenate([kr[:, half:], kr[:, :half]], axis=1)
    w_in_r = jnp.concatenate([uf, cq, ckv, kr, kr_sw, qd, kd, vd, gate], axis=1).astype(bf)
    wq = w_uq[l].reshape(Q_LORA, MLA_HEADS, QK_NOPE + QK_ROPE)
    rope = wq[..., QK_NOPE:]
    rope_sw = jnp.concatenate([rope[..., half:], rope[..., :half]], axis=-1)
    wq = jnp.concatenate([wq, rope_sw], axis=-1).transpose(1, 0, 2).astype(bf)
    wkv = w_ukv[l].reshape(KV_LORA, MLA_HEADS, QK_NOPE + V_HEAD).transpose(1, 0, 2).astype(bf)
    return w_in_r, wq, wkv, w_o[l].astype(bf)


def _encode(x, plan, meta_tokens, bias_tiles, final_norm, norm_w, q_norm, kv_norm, diff_norm,
            lam_vecs, pq, layer_w):
    batch, seq_len, s_pad = plan.batch, plan.seq_len, plan.s_pad
    rows = batch * s_pad
    meta = jnp.broadcast_to(meta_tokens[None], (batch, N_META, D_MODEL))
    pad = jnp.zeros((batch, s_pad - seq_len, D_MODEL), x.dtype)
    h = jnp.concatenate([meta, x, pad], axis=1).reshape(rows, D_MODEL)

    t1 = _rope_table(s_pad)
    dft_tables = _seq_dft_tables(plan)
    for l in range(DEPTH):
        w_in_r, wq, wkv, wo = layer_w[l]
        lam_init = 0.8 - 0.6 * math.exp(-0.3 * l)
        ab, cq, ckv, kr, qd, kd, vd, sg = _in_proj(
            h, norm_w[l][None], w_in_r, pq[l], q_norm[l][None], kv_norm[l][None], t1, plan)
        q, k, v = _mla_up(cq, ckv, kr, t1, wq, wkv, plan)
        yf = _seq_dft(dft_tables, ab, plan)
        ym = _mla_attn(q, k, v, plan)
        yd = _diff_attn(qd, kd, vd, bias_tiles, lam_vecs[l], diff_norm[l][None], plan, lam_init)
        h = _out_proj(h, yf, ym, yd, sg, wo, final_norm[None], plan, final=(l == DEPTH - 1))
    return h


def kernel(x_prompt, x_sample, meta_tokens, rel_bias, final_norm, norm_w, w_in, w_fmix, q_norm,
           w_uq, kv_norm, w_ukv, lam_q1, lam_k1, lam_q2, lam_k2, diff_norm, w_o):
    cc, sc = _dft_angle_tables(F_GROUP_DIM, F_GROUP_DIM, F_GROUP_DIM)
    chan_cs = jnp.stack([cc, -sc]) * (1.0 / math.sqrt(F_GROUP_DIM))
    pq = _fold_fmix(chan_cs, w_fmix).astype(jnp.bfloat16)
    lam_vecs = jnp.stack([lam_q1, lam_k1, lam_q2, lam_k2], axis=1)
    layer_w = [_layer_weights(l, w_in, w_uq, w_ukv, w_o) for l in range(DEPTH)]
    outs = []
    bias_by_tile = {}
    for x in (x_prompt, x_sample):
        plan = _plan(x.shape[0], x.shape[1])
        if plan.tile not in bias_by_tile:
            bias_by_tile[plan.tile] = _bias_tiles(rel_bias, _bucket_tiles(plan.tile))
        outs.append(_encode(
            x, plan, meta_tokens, bias_by_tile[plan.tile], final_norm, norm_w, q_norm, kv_norm,
            diff_norm, lam_vecs, pq, layer_w))
    return tuple(outs)
```

```python
import functools
import math
from typing import NamedTuple

import jax
import jax.numpy as jnp
import numpy as np
from jax import lax
from jax.experimental import pallas as pl
from jax.experimental.pallas import tpu as pltpu

D_MODEL = 2048
DEPTH = 2
N_META = 16
F_GROUPS = 4
F_GROUP_DIM = 128
F_WIDTH = F_GROUPS * F_GROUP_DIM
MLA_HEADS = 8
Q_LORA = 768
KV_LORA = 512
QK_NOPE = 128
QK_ROPE = 64
V_HEAD = 128
MLA_WIDTH = MLA_HEADS * V_HEAD
ROPE_THETA = 10000.0
DIFF_HEADS = 4
DIFF_QK = 64
DIFF_V = 2 * DIFF_QK
DIFF_WIDTH = DIFF_HEADS * DIFF_V
REL_BUCKETS = 32
REL_MAX_DIST = 128
NORM_EPS = 1e-6
DIFF_NORM_EPS = 1e-5
GATE_WIDTH = D_MODEL

LOG2E = math.log2(math.e)
MLA_QSCALE = LOG2E / math.sqrt(QK_NOPE + QK_ROPE)
DIFF_QSCALE = LOG2E / math.sqrt(DIFF_QK)
MASK_VALUE = -1e30

LANES = 128
VMEM_BYTES_V7X = 64 * 1024 * 1024

C_UF = 0
C_CQ = C_UF + F_WIDTH
C_CKV = C_CQ + Q_LORA
C_KR = C_CKV + KV_LORA
C_QD = C_KR + 2 * QK_ROPE
C_KD = C_QD + DIFF_WIDTH
C_VD = C_KD + DIFF_WIDTH
C_GATE = C_VD + DIFF_WIDTH
IN_COLS = C_GATE + GATE_WIDTH

MAX_ROW_TILE = 384
MAX_KEY_CHUNK = 2816
DFT_RADIX = 16
DFT_COL_TILE = 1024
DIFF_SINGLE_CHUNK = 4224
DFT_ROW_TILE = 88
BIAS_SIDE = 2
FINAL_ROW_TILE = 256


class SeqPlan(NamedTuple):
    batch: int
    seq_len: int
    s_pad: int
    tile: int
    chunk: int


def _round_up(n, m):
    return -(-n // m) * m


def _largest_tile(s_pad, limit):
    n = s_pad // LANES
    return LANES * max(d for d in range(1, n + 1) if n % d == 0 and d * LANES <= limit)


def _plan(batch, s_real):
    seq_len = s_real + N_META
    tile = min(MAX_ROW_TILE, _round_up(seq_len, LANES))
    s_pad = _round_up(seq_len, tile)
    return SeqPlan(batch, seq_len, s_pad, tile, _largest_tile(s_pad, MAX_KEY_CHUNK))


def _vmem_limit(nbytes):
    return int(min(nbytes, VMEM_BYTES_V7X - (8 << 20)))


def _rms(x, g, eps):
    ms = jnp.mean(x * x, axis=-1, keepdims=True)
    return x * lax.rsqrt(ms + eps) * g


def _rope_fold(v, t1):
    w = v * t1
    return w + pltpu.roll(w, QK_ROPE, 1)


def _fold_fmix_kernel(cs_ref, w_ref, o_ref):
    w = w_ref[0, 0]
    o_ref[0, 0, :, :F_GROUP_DIM] = jnp.dot(
        cs_ref[0], w, preferred_element_type=jnp.float32,
        precision=lax.Precision.HIGHEST)
    o_ref[0, 0, :, F_GROUP_DIM:] = jnp.dot(
        cs_ref[1], w, preferred_element_type=jnp.float32,
        precision=lax.Precision.HIGHEST)


def _fold_fmix(chan_cs, w_fmix):
    c = F_GROUP_DIM
    return pl.pallas_call(
        _fold_fmix_kernel,
        out_shape=jax.ShapeDtypeStruct((DEPTH, F_GROUPS, c, 2 * c), jnp.float32),
        grid=(DEPTH, F_GROUPS),
        in_specs=[pl.BlockSpec((2, c, c), lambda l, g: (0, 0, 0)),
                  pl.BlockSpec((1, 1, c, c), lambda l, g: (l, g, 0, 0))],
        out_specs=pl.BlockSpec((1, 1, c, 2 * c), lambda l, g: (l, g, 0, 0)),
        name="fold_fmix",
    )(chan_cs, w_fmix)


def _bias_tiles_kernel(tab_ref, bucket_ref, o_ref):
    h = pl.program_id(1)
    bucket = bucket_ref[0]
    acc = jnp.zeros(bucket.shape, jnp.float32)
    for b in range(REL_BUCKETS):
        acc = jnp.where(bucket == b, tab_ref[b * DIFF_HEADS + h], acc)
    o_ref[0, 0] = acc * LOG2E


def _bias_tiles(rel_bias, bucket_tiles):
    nd, t, w = bucket_tiles.shape
    return pl.pallas_call(
        _bias_tiles_kernel,
        out_shape=jax.ShapeDtypeStruct((DIFF_HEADS, nd, t, w), jnp.float32),
        grid=(nd, DIFF_HEADS),
        in_specs=[pl.BlockSpec(memory_space=pltpu.SMEM),
                  pl.BlockSpec((1, t, w), lambda d, h: (d, 0, 0))],
        out_specs=pl.BlockSpec((1, 1, t, w), lambda d, h: (h, d, 0, 0)),
        name="bias_tiles",
    )(rel_bias.reshape(-1), bucket_tiles)


def _in_proj_kernel(x_ref, nw_ref, w_ref, pq_ref, qn_ref, kvn_ref, t1_ref,
                    ab_ref, cq_ref, ckv_ref, kr_ref, qd_ref, kd_ref, vd_ref, sg_ref):
    x = x_ref[...]
    xn = _rms(x, nw_ref[...], NORM_EPS).astype(jnp.bfloat16)

    def proj(c0, width):
        return jnp.dot(xn, w_ref[:, c0:c0 + width], preferred_element_type=jnp.float32)

    uf = proj(C_UF, F_WIDTH).astype(jnp.bfloat16)
    c = F_GROUP_DIM
    for g in range(F_GROUPS):
        ab = jnp.dot(uf[:, g * c:(g + 1) * c], pq_ref[g], preferred_element_type=jnp.float32)
        ab_ref[0, 0, :, g * c:(g + 1) * c] = ab[:, :c].astype(jnp.bfloat16)
        ab_ref[0, 1, :, g * c:(g + 1) * c] = ab[:, c:].astype(jnp.bfloat16)

    cq_ref[...] = _rms(proj(C_CQ, Q_LORA), qn_ref[...], NORM_EPS).astype(jnp.bfloat16)
    ckv_ref[...] = _rms(proj(C_CKV, KV_LORA), kvn_ref[...], NORM_EPS).astype(jnp.bfloat16)
    kr_ref[...] = _rope_fold(proj(C_KR, 2 * QK_ROPE), t1_ref[...]).astype(jnp.bfloat16)
    qd_ref[...] = (proj(C_QD, DIFF_WIDTH) * DIFF_QSCALE).astype(jnp.bfloat16)
    kd_ref[...] = proj(C_KD, DIFF_WIDTH).astype(jnp.bfloat16)
    vd_ref[...] = proj(C_VD, DIFF_WIDTH).astype(jnp.bfloat16)
    step = 512
    for c0 in range(0, GATE_WIDTH, step):
        gate = proj(C_GATE + c0, step)
        sg_ref[:, c0:c0 + step] = (gate * jax.nn.sigmoid(gate)).astype(sg_ref.dtype)


def _in_proj(h, norm_w, w_in, pq, q_norm, kv_norm, t1, plan):
    rows = h.shape[0]
    tm = plan.tile
    nt = plan.s_pad // tm
    row = lambda width: pl.BlockSpec((tm, width), lambda i: (i, 0))
    const = lambda shape: pl.BlockSpec(shape, lambda i: (0,) * len(shape),
                                       pipeline_mode=pl.Buffered(1))
    bf = jnp.bfloat16
    out_shape = (
        jax.ShapeDtypeStruct((plan.batch, 2, plan.s_pad, F_WIDTH), bf),
        jax.ShapeDtypeStruct((rows, Q_LORA), bf),
        jax.ShapeDtypeStruct((rows, KV_LORA), bf),
        jax.ShapeDtypeStruct((rows, 2 * QK_ROPE), bf),
        jax.ShapeDtypeStruct((rows, DIFF_WIDTH), bf),
        jax.ShapeDtypeStruct((rows, DIFF_WIDTH), bf),
        jax.ShapeDtypeStruct((rows, DIFF_WIDTH), bf),
        jax.ShapeDtypeStruct((rows, GATE_WIDTH), bf),
    )
    out_specs = (
        pl.BlockSpec((1, 2, tm, F_WIDTH), lambda i: (i // nt, 0, i % nt, 0)),
        row(Q_LORA), row(KV_LORA), row(2 * QK_ROPE),
        row(DIFF_WIDTH), row(DIFF_WIDTH), row(DIFF_WIDTH), row(GATE_WIDTH),
    )
    in_specs = [
        row(D_MODEL),
        const((1, D_MODEL)),
        const((D_MODEL, IN_COLS)),
        const((F_GROUPS, F_GROUP_DIM, 2 * F_GROUP_DIM)),
        const((1, Q_LORA)),
        const((1, KV_LORA)),
        pl.BlockSpec((tm, 2 * QK_ROPE), lambda i: (i % nt, 0)),
    ]
    return pl.pallas_call(
        _in_proj_kernel,
        out_shape=out_shape,
        grid=(rows // tm,),
        in_specs=in_specs,
        out_specs=out_specs,
        compiler_params=pltpu.CompilerParams(
            dimension_semantics=("arbitrary",), vmem_limit_bytes=_vmem_limit(56 << 20),
            allow_input_fusion=[True, False, False, False, False, False, False]),
        name="in_proj",
    )(h, norm_w, w_in, pq, q_norm, kv_norm, t1)


def _mla_up_kernel(cq_ref, ckv_ref, kr_ref, t1_ref, wq_ref, wkv_ref, q_ref, k_ref, v_ref):
    cq = cq_ref[...]
    ckv = ckv_ref[...]
    t1 = t1_ref[...]
    k_rope = kr_ref[:, :QK_ROPE]
    for h in range(MLA_HEADS):
        r = jnp.dot(cq, wq_ref[h], preferred_element_type=jnp.float32) * MLA_QSCALE
        q_ref[h, :, :QK_NOPE] = r[:, :QK_NOPE].astype(jnp.bfloat16)
        rope = _rope_fold(r[:, QK_NOPE:], t1)
        q_ref[h, :, QK_NOPE:] = rope[:, :QK_ROPE].astype(jnp.bfloat16)
        kv = jnp.dot(ckv, wkv_ref[h], preferred_element_type=jnp.float32)
        k_ref[h, :, :QK_NOPE] = kv[:, :QK_NOPE].astype(jnp.bfloat16)
        k_ref[h, :, QK_NOPE:] = k_rope
        v_ref[h] = kv[:, QK_NOPE:].astype(jnp.bfloat16)


def _mla_up(cq, ckv, kr, t1, wq, wkv, plan):
    rows = cq.shape[0]
    tm = plan.tile
    nt = plan.s_pad // tm
    dqk = QK_NOPE + QK_ROPE
    bf = jnp.bfloat16
    row = lambda width: pl.BlockSpec((tm, width), lambda i: (i, 0))
    heads = lambda width: pl.BlockSpec((MLA_HEADS, tm, width), lambda i: (0, i, 0))
    const = lambda shape: pl.BlockSpec(shape, lambda i: (0,) * len(shape))
    return pl.pallas_call(
        _mla_up_kernel,
        out_shape=(jax.ShapeDtypeStruct((MLA_HEADS, rows, dqk), bf),
                   jax.ShapeDtypeStruct((MLA_HEADS, rows, dqk), bf),
                   jax.ShapeDtypeStruct((MLA_HEADS, rows, V_HEAD), bf)),
        grid=(rows // tm,),
        in_specs=[row(Q_LORA), row(KV_LORA), row(2 * QK_ROPE),
                  pl.BlockSpec((tm, 2 * QK_ROPE), lambda i: (i % nt, 0)),
                  const((MLA_HEADS, Q_LORA, QK_NOPE + 2 * QK_ROPE)),
                  const((MLA_HEADS, KV_LORA, QK_NOPE + V_HEAD))],
        out_specs=(heads(dqk), heads(dqk), heads(V_HEAD)),
        compiler_params=pltpu.CompilerParams(dimension_semantics=("arbitrary",)),
        name="mla_up",
    )(cq, ckv, kr, t1, wq, wkv)


def _dft_inner_kernel(g_ref, x_ref, o_ref):
    a = x_ref[0, 0]
    b = x_ref[0, 1]
    dot = functools.partial(jnp.dot, preferred_element_type=jnp.float32)
    o_ref[0, 0] = dot(g_ref[0, 0], a) + dot(g_ref[0, 1], b)
    o_ref[0, 1] = dot(g_ref[1, 0], a) + dot(g_ref[1, 1], b)


def _dft_inner(g, x):
    batch, _, m2, width = x.shape
    tn = DFT_COL_TILE
    return pl.pallas_call(
        _dft_inner_kernel,
        out_shape=jax.ShapeDtypeStruct((batch, 2, m2, width), jnp.float32),
        grid=(batch, width // tn),
        in_specs=[pl.BlockSpec((2, 2, m2, m2), lambda b, j: (0, 0, 0, 0)),
                  pl.BlockSpec((1, 2, m2, tn), lambda b, j: (b, 0, 0, j))],
        out_specs=pl.BlockSpec((1, 2, m2, tn), lambda b, j: (b, 0, 0, j)),
        compiler_params=pltpu.CompilerParams(
            dimension_semantics=("arbitrary", "arbitrary"),
            vmem_limit_bytes=_vmem_limit(40 << 20)),
        name="dft_inner",
    )(g, x)


def _cmul_const(z, w):
    zr, zi = z
    if w == 1:
        return zr, zi
    if w == -1j:
        return zi, -zr
    wr, wi = float(w.real), float(w.imag)
    return wr * zr - wi * zi, wr * zi + wi * zr


def _fft_real_part(z):
    def fft(z):
        n = len(z)
        if n == 1:
            return z
        even, odd = fft(z[0::2]), fft(z[1::2])
        out = [None] * n
        for k in range(n // 2):
            w = 1 if k == 0 else (-1j if 4 * k == n else np.exp(-2j * np.pi * k / n))
            tr, ti = _cmul_const(odd[k], w)
            out[k] = (even[k][0] + tr, even[k][1] + ti)
            out[k + n // 2] = (even[k][0] - tr, even[k][1] - ti)
        return out

    n = len(z)
    even, odd = fft(z[0::2]), fft(z[1::2])
    out = [None] * n
    for k in range(n // 2):
        w = 1 if k == 0 else (-1j if 4 * k == n else np.exp(-2j * np.pi * k / n))
        tr = _cmul_const(odd[k], w)[0]
        out[k] = even[k][0] + tr
        out[k + n // 2] = even[k][0] - tr
    return out


def _dft_outer_kernel(tw_ref, y_ref, o_ref):
    t2 = y_ref.shape[2]
    tw = tw_ref[...]
    cos = [jnp.broadcast_to(tw[:, n:n + 1], (t2, LANES)) for n in range(DFT_RADIX)]
    sin = [jnp.broadcast_to(tw[:, DFT_RADIX + n:DFT_RADIX + n + 1], (t2, LANES))
           for n in range(DFT_RADIX)]
    for slab in range(F_WIDTH // LANES):
        z = []
        for n1 in range(DFT_RADIX):
            cols = slice(n1 * F_WIDTH + slab * LANES, n1 * F_WIDTH + (slab + 1) * LANES)
            yr, yi = y_ref[0, 0, :, cols], y_ref[0, 1, :, cols]
            z.append((cos[n1] * yr + sin[n1] * yi, cos[n1] * yi - sin[n1] * yr))
        for k1, x in enumerate(_fft_real_part(z)):
            o_ref[0, k1, :, slab * LANES:(slab + 1) * LANES] = x


def _dft_outer(tw, y):
    batch, _, m2, width = y.shape
    t2 = max(d for d in range(8, m2 + 1, 8) if m2 % d == 0 and d <= DFT_ROW_TILE)
    return pl.pallas_call(
        _dft_outer_kernel,
        out_shape=jax.ShapeDtypeStruct((batch, DFT_RADIX, m2, F_WIDTH), jnp.float32),
        grid=(batch, m2 // t2),
        in_specs=[pl.BlockSpec((t2, 2 * DFT_RADIX), lambda b, j: (j, 0)),
                  pl.BlockSpec((1, 2, t2, width), lambda b, j: (b, 0, j, 0))],
        out_specs=pl.BlockSpec((1, DFT_RADIX, t2, F_WIDTH), lambda b, j: (b, 0, j, 0)),
        compiler_params=pltpu.CompilerParams(
            dimension_semantics=("arbitrary", "arbitrary"),
            vmem_limit_bytes=_vmem_limit(40 << 20)),
        name="dft_outer",
    )(tw, y)


def _seq_dft(tables, ab, plan):
    g, tw = tables
    batch, s_pad, n2 = plan.batch, plan.s_pad, plan.seq_len // DFT_RADIX
    m2 = s_pad // DFT_RADIX
    y = _dft_inner(g, ab.reshape(batch, 2, m2, DFT_RADIX * F_WIDTH))
    out = _dft_outer(tw, y)[:, :, :n2].reshape(batch, DFT_RADIX * n2, F_WIDTH)
    out = jnp.pad(out, ((0, 0), (0, s_pad - DFT_RADIX * n2), (0, 0)))
    return out.reshape(batch * s_pad, F_WIDTH)


def _key_chunks(plan, single_chunk_limit=0):
    k_end = _round_up(plan.seq_len, LANES)
    chunk = k_end if k_end <= single_chunk_limit else plan.chunk
    return [(c0, min(chunk, k_end - c0)) for c0 in range(0, k_end, chunk)]


def _mask_pad_keys(s, c0, seq_len):
    width = s.shape[-1]
    n_valid = seq_len - c0
    if n_valid >= width:
        return s
    a = width - LANES
    assert n_valid > a
    lane = lax.broadcasted_iota(jnp.int32, (s.shape[0], LANES), 1)
    tail = jnp.where(lane < n_valid - a, s[:, a:], MASK_VALUE)
    return jnp.concatenate([s[:, :a], tail], axis=1) if a else tail


def _softmax_step(s, vx, state):
    if state is None:
        m_new = jnp.max(s, axis=-1, keepdims=True)
        p = jnp.exp2((s - m_new).astype(jnp.bfloat16))
        return m_new, jnp.dot(p, vx, preferred_element_type=jnp.float32)
    m, acc = state
    m_new = jnp.maximum(m, jnp.max(s, axis=-1, keepdims=True))
    p = jnp.exp2((s - m_new).astype(jnp.bfloat16))
    alpha = jnp.exp2(m - m_new)
    return m_new, alpha * acc + jnp.dot(p, vx, preferred_element_type=jnp.float32)


def _normalized(acc):
    return acc[:, :V_HEAD] / acc[:, V_HEAD:V_HEAD + 1]


def _fill_value_ext(vx_ref, v):
    vx_ref[:, :V_HEAD] = v
    lane = lax.broadcasted_iota(jnp.int32, v.shape, 1)
    vx_ref[:, V_HEAD:] = jnp.where(lane == 0, 1.0, 0.0).astype(vx_ref.dtype)


def _qk(q, k):
    return lax.dot_general(q, k, (((1,), (1,)), ((), ())), preferred_element_type=jnp.float32)


def _tile_rows(tile, tq):
    return pl.ds(pl.multiple_of(tile * tq, tq), tq)


def _attn_pipeline(n_tiles, head, tail):
    def body(i, carry):
        tail(jnp.maximum(i - 1, 0))
        head(i)
        return carry

    lax.fori_loop(0, n_tiles, body, 0)
    tail(n_tiles - 1)


def _init_attn_state(s_last, m_sc, acc_sc):
    s_last[...] = jnp.zeros(s_last.shape, jnp.float32)
    m_sc[...] = jnp.zeros(m_sc.shape, jnp.float32)
    acc_sc[...] = jnp.ones(acc_sc.shape, jnp.float32)


def _store_attn_state(state, m_sc, acc_sc):
    if state is None:
        m_sc[...] = jnp.full(m_sc.shape, MASK_VALUE, jnp.float32)
        acc_sc[...] = jnp.zeros(acc_sc.shape, jnp.float32)
    else:
        m_sc[...], acc_sc[...] = state


def _mla_attn_kernel(q_ref, k_ref, v_ref, o_ref, vx_ref, s_last, m_sc, acc_sc,
                     *, chunks, seq_len, tq):
    c_last, w_last = chunks[-1]
    _init_attn_state(s_last, m_sc, acc_sc)
    _fill_value_ext(vx_ref, v_ref[0])

    def head(tile):
        q = q_ref[0, _tile_rows(tile, tq), :]
        state = None
        for c0, width in chunks[:-1]:
            s = _qk(q, k_ref[0, c0:c0 + width, :])
            state = _softmax_step(s, vx_ref[c0:c0 + width, :], state)
        s = _qk(q, k_ref[0, c_last:c_last + w_last, :])
        s_last[...] = _mask_pad_keys(s, c_last, seq_len)
        _store_attn_state(state, m_sc, acc_sc)

    def tail(tile):
        _, acc = _softmax_step(s_last[...], vx_ref[c_last:c_last + w_last, :],
                               (m_sc[...], acc_sc[...]))
        o_ref[_tile_rows(tile, tq), :] = _normalized(acc).astype(o_ref.dtype)

    _attn_pipeline(q_ref.shape[1] // tq, head, tail)


def _mla_attn(q, k, v, plan):
    tq = plan.tile
    dqk = QK_NOPE + QK_ROPE
    chunks = _key_chunks(plan)
    per_head = lambda width: pl.BlockSpec((1, plan.s_pad, width), lambda b, h: (h, b, 0))
    return pl.pallas_call(
        functools.partial(_mla_attn_kernel, chunks=chunks, seq_len=plan.seq_len, tq=tq),
        out_shape=jax.ShapeDtypeStruct((plan.batch * plan.s_pad, MLA_WIDTH), jnp.bfloat16),
        grid=(plan.batch, MLA_HEADS),
        in_specs=[per_head(dqk), per_head(dqk), per_head(V_HEAD)],
        out_specs=pl.BlockSpec((plan.s_pad, V_HEAD), lambda b, h: (b, h)),
        scratch_shapes=[pltpu.VMEM((plan.s_pad, 2 * V_HEAD), jnp.bfloat16),
                        pltpu.VMEM((tq, chunks[-1][1]), jnp.float32),
                        pltpu.VMEM((tq, 1), jnp.float32),
                        pltpu.VMEM((tq, 2 * V_HEAD), jnp.float32)],
        compiler_params=pltpu.CompilerParams(
            dimension_semantics=("arbitrary", "arbitrary"),
            vmem_limit_bytes=_vmem_limit(56 << 20)),
        name="mla_attn",
    )(q, k, v)


def _diff_attn_kernel(q_ref, k_ref, v_ref, bt_ref, lam_ref, g_ref, o_ref,
                      vx_ref, s_last, m_sc, acc_sc, *, chunks, seq_len, tq, lam_init):
    q_tiles = tq // LANES
    c_last, w_last = chunks[-1]
    _init_attn_state(s_last, m_sc, acc_sc)
    _fill_value_ext(vx_ref, v_ref[...])
    lv = lam_ref[...]
    lam = (jnp.exp(jnp.sum(lv[0:1] * lv[1:2], axis=-1, keepdims=True))
           - jnp.exp(jnp.sum(lv[2:3] * lv[3:4], axis=-1, keepdims=True)) + lam_init)

    def scores(q2x, tile, c0, width):
        s = _qk(q2x, k_ref[c0:c0 + width, :])
        parts = []
        for g in range(width // LANES):
            o = (c0 // LANES + g) - tile * q_tiles
            bias = bt_ref[0, jnp.clip(o, -BIAS_SIDE, q_tiles + BIAS_SIDE - 1) + BIAS_SIDE]
            sub = s[:, g * LANES:(g + 1) * LANES]
            parts.append(jnp.concatenate([sub[:tq] + bias, sub[tq:] + bias], axis=0))
        return jnp.concatenate(parts, axis=1)

    def head(tile):
        qq = q_ref[_tile_rows(tile, tq), :].astype(jnp.float32)
        lane = lax.broadcasted_iota(jnp.int32, qq.shape, 1)
        zero = jnp.zeros_like(qq)
        q2x = jnp.concatenate([jnp.where(lane < DIFF_QK, qq, zero),
                               jnp.where(lane >= DIFF_QK, qq, zero)], axis=0).astype(jnp.bfloat16)
        state = None
        for c0, width in chunks[:-1]:
            state = _softmax_step(scores(q2x, tile, c0, width), vx_ref[c0:c0 + width, :], state)
        s_last[...] = _mask_pad_keys(scores(q2x, tile, c_last, w_last), c_last, seq_len)
        _store_attn_state(state, m_sc, acc_sc)

    def tail(tile):
        _, acc = _softmax_step(s_last[...], vx_ref[c_last:c_last + w_last, :],
                               (m_sc[...], acc_sc[...]))
        o = _normalized(acc)
        od = o[:tq] - lam * o[tq:]
        o_ref[_tile_rows(tile, tq), :] = (
            _rms(od, g_ref[...], DIFF_NORM_EPS) * (1.0 - lam_init)).astype(o_ref.dtype)

    _attn_pipeline(q_ref.shape[0] // tq, head, tail)


def _diff_attn(qd, kd, vd, bias_tiles, lam_vec, diff_norm, plan, lam_init):
    tq = plan.tile
    nd = bias_tiles.shape[1]
    chunks = _key_chunks(plan, DIFF_SINGLE_CHUNK)
    per_head = pl.BlockSpec((plan.s_pad, DIFF_V), lambda b, h: (b, h))
    return pl.pallas_call(
        functools.partial(_diff_attn_kernel, chunks=chunks, seq_len=plan.seq_len, tq=tq,
                          lam_init=lam_init),
        out_shape=jax.ShapeDtypeStruct((plan.batch * plan.s_pad, DIFF_WIDTH), jnp.bfloat16),
        grid=(plan.batch, DIFF_HEADS),
        in_specs=[per_head, per_head, per_head,
                  pl.BlockSpec((1, nd, tq, LANES), lambda b, h: (h, 0, 0, 0)),
                  pl.BlockSpec((4, DIFF_QK), lambda b, h: (0, 0)),
                  pl.BlockSpec((1, DIFF_V), lambda b, h: (0, 0))],
        out_specs=per_head,
        scratch_shapes=[pltpu.VMEM((plan.s_pad, 2 * DIFF_V), jnp.bfloat16),
                        pltpu.VMEM((2 * tq, chunks[-1][1]), jnp.float32),
                        pltpu.VMEM((2 * tq, 1), jnp.float32),
                        pltpu.VMEM((2 * tq, 2 * DIFF_V), jnp.float32)],
        compiler_params=pltpu.CompilerParams(
            dimension_semantics=("arbitrary", "arbitrary"),
            vmem_limit_bytes=_vmem_limit(56 << 20)),
        name="diff_attn",
    )(qd, kd, vd, bias_tiles, lam_vec, diff_norm)


def _out_proj_kernel(x_ref, yf_ref, ym_ref, yd_ref, sg_ref, wo_ref, fn_ref, o_ref, *, final):
    c1 = F_WIDTH
    c2 = F_WIDTH + MLA_WIDTH

    def seg(y, c0, c1_):
        yg = (y.astype(jnp.float32) * sg_ref[:, c0:c1_].astype(jnp.float32)).astype(jnp.bfloat16)
        return jnp.dot(yg, wo_ref[c0:c1_, :], preferred_element_type=jnp.float32)

    o = (x_ref[...] + seg(yf_ref[...], 0, c1) + seg(ym_ref[...], c1, c2)
         + seg(yd_ref[...], c2, D_MODEL))
    if final:
        o = _rms(o, fn_ref[...], NORM_EPS)
    o_ref[...] = o


def _out_proj(h, yf, ym, yd, sg, wo, final_norm, plan, final):
    rows = h.shape[0]
    tm = plan.tile
    row = lambda width: pl.BlockSpec((tm, width), lambda i: (i, 0))
    const = lambda shape: pl.BlockSpec(shape, lambda *_: (0,) * len(shape),
                                       pipeline_mode=pl.Buffered(1))
    params = pltpu.CompilerParams(
        vmem_limit_bytes=_vmem_limit(48 << 20),
        allow_input_fusion=[False, True, False, False, False, False, False])
    args = (h, yf, ym, yd, sg, wo, final_norm)
    widths = (D_MODEL, F_WIDTH, MLA_WIDTH, DIFF_WIDTH, GATE_WIDTH)
    consts = [const((D_MODEL, D_MODEL)), const((1, D_MODEL))]
    if not final:
        return pl.pallas_call(
            functools.partial(_out_proj_kernel, final=False),
            out_shape=jax.ShapeDtypeStruct((rows, D_MODEL), jnp.float32),
            grid=(rows // tm,),
            in_specs=[row(w) for w in widths] + consts,
            out_specs=row(D_MODEL),
            compiler_params=params,
            name="out_proj",
        )(*args)
    s_real = plan.seq_len - N_META
    tf = _largest_tile(s_real, FINAL_ROW_TILE)
    shifted = lambda width: pl.BlockSpec(
        (pl.Element(tf), pl.Element(width)),
        lambda b, j: (pl.multiple_of(b * plan.s_pad + N_META + j * tf, N_META), 0))
    return pl.pallas_call(
        functools.partial(_out_proj_kernel, final=True),
        out_shape=jax.ShapeDtypeStruct((plan.batch, s_real, D_MODEL), jnp.float32),
        grid=(plan.batch, s_real // tf),
        in_specs=[shifted(w) for w in widths] + consts,
        out_specs=pl.BlockSpec((None, tf, D_MODEL), lambda b, j: (b, j, 0)),
        compiler_params=params,
        name="out_proj_final",
    )(*args)


def _t5_bucket(rel):
    nb = REL_BUCKETS // 2
    max_exact = nb // 2
    ret = (rel > 0).astype(jnp.int32) * nb
    n = jnp.abs(rel)
    nf = jnp.maximum(n, 1).astype(jnp.float32)
    large = max_exact + (jnp.log(nf / max_exact) / math.log(REL_MAX_DIST / max_exact)
                         * (nb - max_exact)).astype(jnp.int32)
    large = jnp.minimum(large, nb - 1)
    return ret + jnp.where(n < max_exact, n, large)


def _bucket_tiles(tq):
    assert LANES >= REL_MAX_DIST
    o = jnp.arange(-BIAS_SIDE, tq // LANES + BIAS_SIDE, dtype=jnp.int32)[:, None, None] * LANES
    row = jnp.arange(tq, dtype=jnp.int32)[None, :, None]
    col = jnp.arange(LANES, dtype=jnp.int32)[None, None, :]
    return _t5_bucket(o + col - row)


def _rope_table(s_pad):
    pos = jnp.arange(s_pad, dtype=jnp.float32)
    inv_freq = ROPE_THETA ** (-jnp.arange(0, QK_ROPE, 2, dtype=jnp.float32) / QK_ROPE)
    ang = pos[:, None] * inv_freq[None, :]
    cos, sin = jnp.cos(ang), jnp.sin(ang)
    return jnp.concatenate([cos, cos, -sin, sin], axis=-1)


def _dft_angle_tables(n, rows, cols):
    r = (jnp.arange(rows, dtype=jnp.int32)[:, None] * jnp.arange(cols, dtype=jnp.int32)[None, :]) % n
    ang = r.astype(jnp.float32) * (2.0 * math.pi / n)
    return jnp.cos(ang), jnp.sin(ang)


def _seq_dft_tables(plan):
    seq_len = plan.seq_len
    assert seq_len % DFT_RADIX == 0 and plan.s_pad % DFT_RADIX == 0
    n2 = seq_len // DFT_RADIX
    m2 = plan.s_pad // DFT_RADIX
    c, s = _dft_angle_tables(n2, m2, m2)
    idx = jnp.arange(m2, dtype=jnp.int32)
    valid = (idx[:, None] < n2) & (idx[None, :] < n2)
    zero = jnp.zeros((), jnp.float32)
    c = jnp.where(valid, c, zero)
    s = jnp.where(valid, s, zero)
    g = jnp.stack([jnp.stack([c, s]), jnp.stack([-s, c])]).astype(jnp.bfloat16)
    cw, sw = _dft_angle_tables(seq_len, m2, DFT_RADIX)
    tw = jnp.concatenate([cw, sw], axis=-1) * (1.0 / math.sqrt(seq_len))
    return g, tw


def _layer_weights(l, w_in, w_uq, w_ukv, w_o):
    bf = jnp.bfloat16
    w = w_in[l]
    s = np.cumsum([0, F_WIDTH, Q_LORA, KV_LORA, QK_ROPE, DIFF_WIDTH, DIFF_WIDTH, DIFF_WIDTH,
                   GATE_WIDTH])
    uf, cq, ckv, kr, qd, kd, vd, gate = (w[:, s[i]:s[i + 1]] for i in range(8))
    half = QK_ROPE // 2
    kr_sw = jnp.concatenate([kr[:, half:], kr[:, :half]], axis=1)
    w_in_r = jnp.concatenate([uf, cq, ckv, kr, kr_sw, qd, kd, vd, gate], axis=1).astype(bf)
    wq = w_uq[l].reshape(Q_LORA, MLA_HEADS, QK_NOPE + QK_ROPE)
    rope = wq[..., QK_NOPE:]
    rope_sw = jnp.concatenate([rope[..., half:], rope[..., :half]], axis=-1)
    wq = jnp.concatenate([wq, rope_sw], axis=-1).transpose(1, 0, 2).astype(bf)
    wkv = w_ukv[l].reshape(KV_LORA, MLA_HEADS, QK_NOPE + V_HEAD).transpose(1, 0, 2).astype(bf)
    return w_in_r, wq, wkv, w_o[l].astype(bf)


def _encode(x, plan, meta_tokens, bias_tiles, final_norm, norm_w, q_norm, kv_norm, diff_norm,
            lam_vecs, pq, layer_w):
    batch, seq_len, s_pad = plan.batch, plan.seq_len, plan.s_pad
    rows = batch * s_pad
    meta = jnp.broadcast_to(meta_tokens[None], (batch, N_META, D_MODEL))
    pad = jnp.zeros((batch, s_pad - seq_len, D_MODEL), x.dtype)
    h = jnp.concatenate([meta, x, pad], axis=1).reshape(rows, D_MODEL)

    t1 = _rope_table(s_pad)
    dft_tables = _seq_dft_tables(plan)
    for l in range(DEPTH):
        w_in_r, wq, wkv, wo = layer_w[l]
        lam_init = 0.8 - 0.6 * math.exp(-0.3 * l)
        ab, cq, ckv, kr, qd, kd, vd, sg = _in_proj(
            h, norm_w[l][None], w_in_r, pq[l], q_norm[l][None], kv_norm[l][None], t1, plan)
        q, k, v = _mla_up(cq, ckv, kr, t1, wq, wkv, plan)
        yf = _seq_dft(dft_tables, ab, plan)
        ym = _mla_attn(q, k, v, plan)
        yd = _diff_attn(qd, kd, vd, bias_tiles, lam_vecs[l], diff_norm[l][None], plan, lam_init)
        h = _out_proj(h, yf, ym, yd, sg, wo, final_norm[None], plan, final=(l == DEPTH - 1))
    return h


def kernel(x_prompt, x_sample, meta_tokens, rel_bias, final_norm, norm_w, w_in, w_fmix, q_norm,
           w_uq, kv_norm, w_ukv, lam_q1, lam_k1, lam_q2, lam_k2, diff_norm, w_o):
    cc, sc = _dft_angle_tables(F_GROUP_DIM, F_GROUP_DIM, F_GROUP_DIM)
    chan_cs = jnp.stack([cc, -sc]) * (1.0 / math.sqrt(F_GROUP_DIM))
    pq = _fold_fmix(chan_cs, w_fmix).astype(jnp.bfloat16)
    lam_vecs = jnp.stack([lam_q1, lam_k1, lam_q2, lam_k2], axis=1)
    layer_w = [_layer_weights(l, w_in, w_uq, w_ukv, w_o) for l in range(DEPTH)]
    outs = []
    bias_by_tile = {}
    for x in (x_prompt, x_sample):
        plan = _plan(x.shape[0], x.shape[1])
        if plan.tile not in bias_by_tile:
            bias_by_tile[plan.tile] = _bias_tiles(rel_bias, _bucket_tiles(plan.tile))
        outs.append(_encode(
            x, plan, meta_tokens, bias_by_tile[plan.tile], final_norm, norm_w, q_norm, kv_norm,
            diff_norm, lam_vecs, pq, layer_w))
    return tuple(outs)
```
